```python
import math
import jax, jax.numpy as jnp
from jax import lax
import numpy as np

D_MODEL = 1024
BATCH = 16
SEQ = 4096
DEPTH = 4

N_MIXERS = 3
CHUNK = 64
D_FF = 2816
LN_EPS = 1e-5
NORM_EPS = 1e-6
DN_ALPHA = (2 * DEPTH) ** 0.25
DN_BETA = (8 * DEPTH) ** -0.25

GLA_HEADS = 4
GLA_DK = D_MODEL // (2 * GLA_HEADS)
GLA_DV = D_MODEL // GLA_HEADS
GLA_KEY = GLA_HEADS * GLA_DK
GLA_VAL = GLA_HEADS * GLA_DV
GLA_RANK = 16
GLA_TAU = 16.0
GLA_SPLITS = (GLA_KEY, GLA_KEY, GLA_VAL, GLA_VAL, GLA_RANK)

RET_HEADS = 4
RET_DK = D_MODEL // RET_HEADS
RET_DV = 2 * D_MODEL // RET_HEADS
RET_KEY = RET_HEADS * RET_DK
RET_VAL = RET_HEADS * RET_DV
RET_SPLITS = (RET_KEY, RET_KEY, RET_VAL, RET_VAL)
ROPE_BASE = 10000.0

SSD_INNER = 2 * D_MODEL
SSD_HEADDIM = 64
SSD_HEADS = SSD_INNER // SSD_HEADDIM
SSD_GROUPS = 4
SSD_HPG = SSD_HEADS // SSD_GROUPS
SSD_STATE = 128
SSD_CONV = 4
SSD_CONV_DIM = SSD_INNER + 2 * SSD_GROUPS * SSD_STATE
SSD_SPLITS = (SSD_INNER, SSD_CONV_DIM, SSD_HEADS)

N_LAYERS_A = len(range(0, DEPTH, N_MIXERS))
N_LAYERS_B = len(range(1, DEPTH, N_MIXERS))
N_LAYERS_C = len(range(2, DEPTH, N_MIXERS))

kernel_name = 'hybrid_gla_retnet_ssd_macaron_deepnorm'


def split_cols(t, sizes):
    return jnp.split(t, np.cumsum(sizes)[:-1].tolist(), axis=-1)


def layer_norm(x, g, b):
    xf = x.astype(jnp.float32)
    mu = xf.mean(-1, keepdims=True)
    var = jnp.square(xf - mu).mean(-1, keepdims=True)
    return ((xf - mu) * lax.rsqrt(var + LN_EPS)).astype(x.dtype) * g + b


def rms_norm(x, g):
    xf = x.astype(jnp.float32)
    ms = jnp.square(xf).mean(-1, keepdims=True)
    return (xf * lax.rsqrt(ms + NORM_EPS)).astype(x.dtype) * g


def head_group_norm(x, g):
    xf = x.astype(jnp.float32)
    mu = xf.mean(-1, keepdims=True)
    var = jnp.square(xf - mu).mean(-1, keepdims=True)
    return ((xf - mu) * lax.rsqrt(var + NORM_EPS)).astype(x.dtype) * g


def swiglu(x, w_in, w_out):
    gate, up = jnp.split(x @ w_in, 2, axis=-1)
    return (jax.nn.silu(gate) * up) @ w_out


def to_chunks(t):
    b, l = t.shape[:2]
    t = t.reshape((b, l // CHUNK, CHUNK) + t.shape[2:])
    return jnp.moveaxis(t, 1, 0)


def from_chunks(t):
    t = jnp.moveaxis(t, 0, 1)
    return t.reshape((t.shape[0], t.shape[1] * t.shape[2]) + t.shape[3:])


def rotary(t):
    l, dh = t.shape[1], t.shape[-1]
    inv = ROPE_BASE ** (-jnp.arange(0, dh, 2, dtype=jnp.float32) / dh)
    ang = jnp.arange(l, dtype=jnp.float32)[:, None] * inv[None, :]
    cos = jnp.cos(ang)[None, :, None, :].astype(t.dtype)
    sin = jnp.sin(ang)[None, :, None, :].astype(t.dtype)
    t1, t2 = jnp.split(t, 2, axis=-1)
    return jnp.concatenate([t1 * cos - t2 * sin, t1 * sin + t2 * cos], axis=-1)


def gla_chunked(q, k, v, log_a):
    bsz = q.shape[0]
    mask = jnp.tril(jnp.ones((CHUNK, CHUNK), dtype=bool))

    def step(state, inp):
        qc, kc, vc, ac = inp
        cum = jnp.cumsum(ac, axis=1)
        tot = cum[:, -1]
        q_dec = qc * jnp.exp(cum)
        k_inv = kc * jnp.exp(-cum)
        k_end = kc * jnp.exp(tot[:, None] - cum)
        s = jnp.where(mask, jnp.einsum('bihd,bjhd->bhij', q_dec, k_inv), 0.0)
        o = jnp.einsum('bhij,bjhv->bihv', s, vc) + jnp.einsum('bihd,bhdv->bihv', q_dec, state)
        state = jnp.exp(tot)[..., None] * state + jnp.einsum('bjhd,bjhv->bhdv', k_end, vc)
        return state, o

    state0 = jnp.zeros((bsz, GLA_HEADS, GLA_DK, GLA_DV), jnp.float32)
    _, o = lax.scan(step, state0, (to_chunks(q), to_chunks(k), to_chunks(v), to_chunks(log_a)))
    return from_chunks(o).astype(v.dtype)


def gla_mixer(x, w_in, w_gate, b_gate, norm_g, w_out):
    bsz, l, _ = x.shape
    q, k, v, r, g_low = split_cols(x @ w_in, GLA_SPLITS)
    q = q.reshape(bsz, l, GLA_HEADS, GLA_DK) * GLA_DK ** -0.5
    k = k.reshape(bsz, l, GLA_HEADS, GLA_DK)
    v = v.reshape(bsz, l, GLA_HEADS, GLA_DV)
    log_a = jax.nn.log_sigmoid((g_low @ w_gate + b_gate).astype(jnp.float32)) / GLA_TAU
    log_a = log_a.reshape(bsz, l, GLA_HEADS, GLA_DK)
    o = rms_norm(gla_chunked(q, k, v, log_a), norm_g)
    return (o.reshape(bsz, l, GLA_VAL) * jax.nn.silu(r)) @ w_out


def retention_chunked(q, k, v):
    bsz = q.shape[0]
    log_g = jnp.log(1.0 - 2.0 ** (-5.0 - jnp.arange(RET_HEADS, dtype=jnp.float32)))
    idx = jnp.arange(CHUNK, dtype=jnp.float32)
    mask = jnp.tril(jnp.ones((CHUNK, CHUNK), dtype=bool))
    decay_intra = jnp.where(mask[None], jnp.exp((idx[:, None] - idx[None, :])[None] * log_g[:, None, None]), 0.0)
    decay_q = jnp.exp((idx + 1.0)[:, None] * log_g[None, :])[None, :, :, None]
    decay_k = jnp.exp((CHUNK - 1.0 - idx)[:, None] * log_g[None, :])[None, :, :, None]
    decay_chunk = jnp.exp(CHUNK * log_g)[None, :, None, None]

    def step(state, inp):
        qc, kc, vc = inp
        s = jnp.einsum('bihd,bjhd->bhij', qc, kc) * decay_intra
        o = jnp.einsum('bhij,bjhv->bihv', s, vc) + jnp.einsum('bihd,bhdv->bihv', qc, state) * decay_q
        state = decay_chunk * state + jnp.einsum('bjhd,bjhv->bhdv', kc * decay_k, vc)
        return state, o

    state0 = jnp.zeros((bsz, RET_HEADS, RET_DK, RET_DV), jnp.float32)
    _, o = lax.scan(step, state0, (to_chunks(q), to_chunks(k), to_chunks(v)))
    return from_chunks(o).astype(v.dtype)


def retention_mixer(x, w_in, norm_g, w_out):
    bsz, l, _ = x.shape
    q, k, v, g = split_cols(x @ w_in, RET_SPLITS)
    q = rotary(q.reshape(bsz, l, RET_HEADS, RET_DK))
    k = rotary(k.reshape(bsz, l, RET_HEADS, RET_DK)) * RET_DK ** -0.5
    v = v.reshape(bsz, l, RET_HEADS, RET_DV)
    o = head_group_norm(retention_chunked(q, k, v), norm_g)
    return (o.reshape(bsz, l, RET_VAL) * jax.nn.silu(g)) @ w_out


def causal_dwconv(t, w, b):
    ch = t.shape[-1]
    y = lax.conv_general_dilated(t, w[:, None, :].astype(t.dtype), window_strides=(1,),
                                 padding=[(SSD_CONV - 1, 0)],
                                 dimension_numbers=('NWC', 'WIO', 'NWC'),
                                 feature_group_count=ch)
    return y + b


def ssd_chunked(xs, dt, a, bm, cm):
    bsz = xs.shape[0]
    mask = jnp.tril(jnp.ones((CHUNK, CHUNK), dtype=bool))[None, :, :, None, None]

    def step(state, inp):
        xc, dtc, bc, cc = inp
        cum = jnp.cumsum(dtc * a, axis=1)
        seg = jnp.exp(jnp.where(mask, cum[:, :, None] - cum[:, None, :], -jnp.inf))
        cb = jnp.einsum('bign,bjgn->bijg', cc, bc)
        w = seg * cb[..., None] * dtc[:, None]
        y = jnp.einsum('bijge,bjgep->bigep', w, xc)
        y = y + jnp.einsum('bign,bgepn->bigep', cc, state) * jnp.exp(cum)[..., None]
        to_end = jnp.exp(cum[:, -1:] - cum) * dtc
        state = jnp.exp(cum[:, -1])[..., None, None] * state + jnp.einsum('bjgn,bjge,bjgep->bgepn', bc, to_end, xc)
        return state, y

    state0 = jnp.zeros((bsz, SSD_GROUPS, SSD_HPG, SSD_HEADDIM, SSD_STATE), jnp.float32)
    _, y = lax.scan(step, state0, (to_chunks(xs), to_chunks(dt), to_chunks(bm), to_chunks(cm)))
    return from_chunks(y).astype(xs.dtype)


def ssd_mixer(x, w_in, conv_w, conv_b, dt_bias, a_log, d_skip, norm_g, w_out):
    bsz, l, _ = x.shape
    z, xbc, dt_raw = split_cols(x @ w_in, SSD_SPLITS)
    xbc = jax.nn.silu(causal_dwconv(xbc, conv_w, conv_b))
    xs, bm, cm = split_cols(xbc, (SSD_INNER, SSD_GROUPS * SSD_STATE, SSD_GROUPS * SSD_STATE))
    xs = xs.reshape(bsz, l, SSD_GROUPS, SSD_HPG, SSD_HEADDIM)
    bm = bm.reshape(bsz, l, SSD_GROUPS, SSD_STATE)
    cm = cm.reshape(bsz, l, SSD_GROUPS, SSD_STATE)
    dt = jax.nn.softplus((dt_raw + dt_bias).astype(jnp.float32)).reshape(bsz, l, SSD_GROUPS, SSD_HPG)
    a = -jnp.exp(a_log.astype(jnp.float32)).reshape(SSD_GROUPS, SSD_HPG)
    y = ssd_chunked(xs, dt, a, bm, cm) + xs * d_skip.reshape(SSD_GROUPS, SSD_HPG)[..., None]
    y = y.reshape(bsz, l, SSD_INNER) * jax.nn.silu(z)
    y = rms_norm(y.reshape(bsz, l, SSD_GROUPS, SSD_INNER // SSD_GROUPS), norm_g.reshape(SSD_GROUPS, -1))
    return y.reshape(bsz, l, SSD_INNER) @ w_out


def _col_scale(sizes, scaled):
    return jnp.concatenate([jnp.full((s,), DN_BETA if i in scaled else 1.0, jnp.float32)
                            for i, s in enumerate(sizes)])


def setup_inputs(seed: int = 0) -> dict:
    key = jax.random.key(seed)
    ks = list(jax.random.split(key, 24))

    def nrm(i, shape, scale):
        return jax.random.normal(ks[i], shape, jnp.float32) * scale

    x = nrm(0, (BATCH, SEQ, D_MODEL), 1.0)
    ffn_w_in = nrm(1, (DEPTH, 2, D_MODEL, 2 * D_FF), DN_BETA * D_MODEL ** -0.5)
    ffn_w_out = nrm(2, (DEPTH, 2, D_FF, D_MODEL), DN_BETA * D_FF ** -0.5)
    ln_g = 1.0 + nrm(3, (DEPTH, 3, D_MODEL), 0.02)
    ln_b = nrm(4, (DEPTH, 3, D_MODEL), 0.02)
    gla_w_in = nrm(5, (N_LAYERS_A, D_MODEL, sum(GLA_SPLITS)), D_MODEL ** -0.5) * _col_scale(GLA_SPLITS, (2,))
    gla_w_gate = nrm(6, (N_LAYERS_A, GLA_RANK, GLA_KEY), GLA_RANK ** -0.5)
    gla_b_gate = nrm(7, (N_LAYERS_A, GLA_KEY), 0.1)
    gla_norm_g = 1.0 + nrm(8, (N_LAYERS_A, GLA_DV), 0.02)
    gla_w_out = nrm(9, (N_LAYERS_A, GLA_VAL, D_MODEL), DN_BETA * GLA_VAL ** -0.5)
    ret_w_in = nrm(10, (N_LAYERS_B, D_MODEL, sum(RET_SPLITS)), D_MODEL ** -0.5) * _col_scale(RET_SPLITS, (2,))
    ret_norm_g = 1.0 + nrm(11, (N_LAYERS_B, RET_DV), 0.02)
    ret_w_out = nrm(12, (N_LAYERS_B, RET_VAL, D_MODEL), DN_BETA * RET_VAL ** -0.5)
    ssd_cols = (SSD_INNER, SSD_INNER, SSD_CONV_DIM - SSD_INNER, SSD_HEADS)
    ssd_w_in = nrm(13, (N_LAYERS_C, D_MODEL, sum(SSD_SPLITS)), D_MODEL ** -0.5) * _col_scale(ssd_cols, (1,))
    ssd_conv_w = nrm(14, (N_LAYERS_C, SSD_CONV, SSD_CONV_DIM), SSD_CONV ** -0.5)
    ssd_conv_b = nrm(15, (N_LAYERS_C, SSD_CONV_DIM), 0.02)
    dt0 = jnp.exp(jax.random.uniform(ks[16], (N_LAYERS_C, SSD_HEADS), jnp.float32,
                                     math.log(1e-3), math.log(1e-1)))
    ssd_dt_bias = dt0 + jnp.log(-jnp.expm1(-dt0))
    ssd_a_log = jnp.log(jax.random.uniform(ks[17], (N_LAYERS_C, SSD_HEADS), jnp.float32, 1.0, 16.0))
    ssd_d = 1.0 + nrm(18, (N_LAYERS_C, SSD_HEADS), 0.02)
    ssd_norm_g = 1.0 + nrm(19, (N_LAYERS_C, SSD_INNER), 0.02)
    ssd_w_out = nrm(20, (N_LAYERS_C, SSD_INNER, D_MODEL), DN_BETA * SSD_INNER ** -0.5)
    return {'x': x, 'ffn_w_in': ffn_w_in, 'ffn_w_out': ffn_w_out, 'ln_g': ln_g, 'ln_b': ln_b,
            'gla_w_in': gla_w_in, 'gla_w_gate': gla_w_gate, 'gla_b_gate': gla_b_gate,
            'gla_norm_g': gla_norm_g, 'gla_w_out': gla_w_out,
            'ret_w_in': ret_w_in, 'ret_norm_g': ret_norm_g, 'ret_w_out': ret_w_out,
            'ssd_w_in': ssd_w_in, 'ssd_conv_w': ssd_conv_w, 'ssd_conv_b': ssd_conv_b,
            'ssd_dt_bias': ssd_dt_bias, 'ssd_a_log': ssd_a_log, 'ssd_d': ssd_d,
            'ssd_norm_g': ssd_norm_g, 'ssd_w_out': ssd_w_out}


def reference(x, ffn_w_in, ffn_w_out, ln_g, ln_b,
              gla_w_in, gla_w_gate, gla_b_gate, gla_norm_g, gla_w_out,
              ret_w_in, ret_norm_g, ret_w_out,
              ssd_w_in, ssd_conv_w, ssd_conv_b, ssd_dt_bias, ssd_a_log, ssd_d, ssd_norm_g, ssd_w_out):
    h = x
    for i in range(DEPTH):
        h = layer_norm(DN_ALPHA * h + 0.5 * swiglu(h, ffn_w_in[i, 0], ffn_w_out[i, 0]), ln_g[i, 0], ln_b[i, 0])
        kind, j = i % N_MIXERS, i // N_MIXERS
        if kind == 0:
            m = gla_mixer(h, gla_w_in[j], gla_w_gate[j], gla_b_gate[j], gla_norm_g[j], gla_w_out[j])
        elif kind == 1:
            m = retention_mixer(h, ret_w_in[j], ret_norm_g[j], ret_w_out[j])
        else:
            m = ssd_mixer(h, ssd_w_in[j], ssd_conv_w[j], ssd_conv_b[j], ssd_dt_bias[j],
                          ssd_a_log[j], ssd_d[j], ssd_norm_g[j], ssd_w_out[j])
        h = layer_norm(DN_ALPHA * h + m, ln_g[i, 1], ln_b[i, 1])
        h = layer_norm(DN_ALPHA * h + 0.5 * swiglu(h, ffn_w_in[i, 1], ffn_w_out[i, 1]), ln_g[i, 2], ln_b[i, 2])
    return h
```

```python
import functools
import math

import jax
import jax.numpy as jnp
from jax import lax
from jax.experimental import pallas as pl
from jax.experimental.pallas import tpu as pltpu

BF = jnp.bfloat16
F32 = jnp.float32

LANES = 128
SUBLANES = 8
VMEM_LIMIT = 56 * 1024 * 1024

DEPTH = 4
N_MIXERS = 3
CHUNK = 64
LN_EPS = 1e-5
NORM_EPS = 1e-6
DN_ALPHA = (2 * DEPTH) ** 0.25

GLA_HEADS = 4
GLA_TAU = 16.0
RET_HEADS = 4
ROPE_BASE = 10000.0
SSD_HEADDIM = 64
SSD_GROUPS = 4
SSD_STATE = 128
SSD_CONV = 4

FFN_TILE = 256
MIX_TILE = 256


def _dot(a, b):
    return jnp.dot(a, b, preferred_element_type=F32)


def _dot_nt(a, b):
    return lax.dot_general(a, b, (((1,), (1,)), ((), ())), preferred_element_type=F32)


def _dot_tn(a, b):
    return lax.dot_general(a, b, (((0,), (0,)), ((), ())), preferred_element_type=F32)


def _split_bf16(x, parts):
    out = []
    r = x
    for i in range(parts):
        p = r.astype(BF)
        out.append(p)
        if i + 1 < parts:
            r = r - p.astype(F32)
    return out


def _sel_dot_lhs(sel, x, parts):
    acc = None
    for p in _split_bf16(x, parts):
        t = _dot(sel, p)
        acc = t if acc is None else acc + t
    return acc


def _sel_dot_rhs(x, sel, parts):
    acc = None
    for p in _split_bf16(x, parts):
        t = _dot(p, sel)
        acc = t if acc is None else acc + t
    return acc


def _sigmoid(x):
    return 1.0 / (1.0 + jnp.exp(-x))


def _silu(x):
    return x * _sigmoid(x)


def _softplus(x):
    return jnp.maximum(x, 0.0) + jnp.log(1.0 + jnp.exp(-jnp.abs(x)))


def _layer_norm(z, g, b):
    mu = jnp.mean(z, axis=-1, keepdims=True)
    zc = z - mu
    var = jnp.mean(zc * zc, axis=-1, keepdims=True)
    return zc * lax.rsqrt(var + LN_EPS) * g + b


def _chunk_tril(tile, chunk):
    ri = lax.broadcasted_iota(jnp.int32, (tile, tile), 0)
    ci = lax.broadcasted_iota(jnp.int32, (tile, tile), 1)
    same = (ri // chunk) == (ci // chunk)
    return jnp.where(same & (ci <= ri), 1.0, 0.0).astype(BF)


def _resident(shape):
    nd = len(shape)
    return pl.BlockSpec(shape, lambda *_: (0,) * nd, pipeline_mode=pl.Buffered(1))


def _ffn_body(x_ref, w_in_ref, w_out_ref, g_ref, b_ref, o_ref, *, d_ff):
    x = x_ref[...]
    xb = x.astype(BF)
    gate = _dot(xb, w_in_ref[:, :d_ff])
    up = _dot(xb, w_in_ref[:, d_ff:])
    act = (_silu(gate) * up).astype(BF)
    y = _dot(act, w_out_ref[...])
    o_ref[...] = _layer_norm(DN_ALPHA * x + 0.5 * y, g_ref[...], b_ref[...])


def _ffn_layer(h, w_in, w_out, g, b):
    t, d = h.shape
    d_ff = w_out.shape[0]
    tm = min(FFN_TILE, t)
    return pl.pallas_call(
        functools.partial(_ffn_body, d_ff=d_ff),
        grid=(t // tm,),
        in_specs=[
            pl.BlockSpec((tm, d), lambda i: (i, 0)),
            _resident((d, 2 * d_ff)),
            _resident((d_ff, d)),
            _resident((1, d)),
            _resident((1, d)),
        ],
        out_specs=pl.BlockSpec((tm, d), lambda i: (i, 0)),
        out_shape=jax.ShapeDtypeStruct((t, d), F32),
        compiler_params=pltpu.CompilerParams(
            dimension_semantics=("parallel",), vmem_limit_bytes=VMEM_LIMIT),
        name="ffn_ln",
    )(h, w_in.astype(BF), w_out.astype(BF), g.reshape(1, d), b.reshape(1, d))


def _gla_body(x_ref, w_in_ref, w_gate_ref, b_gate_ref, ng_ref, w_out_ref, lg_ref, lb_ref,
              o_ref, st_ref, ov_ref, *, key, val, heads):
    dk = key // heads
    dv = val // heads
    tile = x_ref.shape[0]

    @pl.when(pl.program_id(1) == 0)
    def _():
        st_ref[...] = jnp.zeros_like(st_ref)

    x = x_ref[...]
    xb = x.astype(BF)
    q = _dot(xb, w_in_ref[:, 0:key]) * (dk ** -0.5)
    k = _dot(xb, w_in_ref[:, key:2 * key])
    v = _dot(xb, w_in_ref[:, 2 * key:2 * key + val]).astype(BF)
    g_low = _dot(xb, w_in_ref[:, 2 * key + 2 * val:2 * key + 2 * val + LANES]).astype(BF)
    zg = _dot(g_low, w_gate_ref[...]) + b_gate_ref[...]
    log_a = (jnp.minimum(zg, 0.0) - jnp.log(1.0 + jnp.exp(-jnp.abs(zg)))) * (1.0 / GLA_TAU)
    cum = _sel_dot_lhs(_chunk_tril(tile, CHUNK), log_a, 3)

    ri = lax.broadcasted_iota(jnp.int32, (CHUNK, CHUNK), 0)
    ci = lax.broadcasted_iota(jnp.int32, (CHUNK, CHUNK), 1)
    causal = ci <= ri

    for c in range(tile // CHUNK):
        rows = slice(c * CHUNK, (c + 1) * CHUNK)
        cum_c = cum[rows]
        tot = cum_c[CHUNK - 1:CHUNK, :]
        q_dec = (q[rows] * jnp.exp(cum_c)).astype(BF)
        k_inv = (k[rows] * jnp.exp(-cum_c)).astype(BF)
        k_end = (k[rows] * jnp.exp(tot - cum_c)).astype(BF)
        dec = jnp.exp(tot)
        for h in range(heads):
            kl = slice(h * dk, (h + 1) * dk)
            vl = slice(h * dv, (h + 1) * dv)
            s = jnp.where(causal, _dot_nt(q_dec[:, kl], k_inv[:, kl]), 0.0).astype(BF)
            st = st_ref[h]
            o = _dot(s, v[rows, vl]) + _dot_nt(q_dec[:, kl], st.astype(BF))
            st_ref[h] = st * dec[:, kl] + _dot_tn(v[rows, vl], k_end[:, kl])
            ov_ref[rows, vl] = o

    r = _dot(xb, w_in_ref[:, 2 * key + val:2 * key + 2 * val])
    ng = ng_ref[...]
    normed = []
    for h in range(heads):
        oh = ov_ref[:, h * dv:(h + 1) * dv]
        ms = jnp.mean(oh * oh, axis=-1, keepdims=True)
        normed.append(oh * lax.rsqrt(ms + NORM_EPS) * ng)
    y = (jnp.concatenate(normed, axis=1) * _silu(r)).astype(BF)
    m = _dot(y, w_out_ref[...])
    o_ref[...] = _layer_norm(DN_ALPHA * x + m, lg_ref[...], lb_ref[...])


def _gla_layer(h, w_in, w_gate, b_gate, norm_g, w_out, ln_g, ln_b):
    b, l, d = h.shape
    rank, key = w_gate.shape
    val = w_out.shape[0]
    heads = GLA_HEADS
    tile = min(MIX_TILE, l)
    w_in_p = jnp.pad(w_in, ((0, 0), (0, LANES - rank))).astype(BF)
    w_gate_p = jnp.pad(w_gate, ((0, LANES - rank), (0, 0))).astype(BF)
    n_in = w_in_p.shape[1]
    body = functools.partial(_gla_body, key=key, val=val, heads=heads)
    return pl.pallas_call(
        body,
        grid=(b, l // tile),
        in_specs=[
            pl.BlockSpec((None, tile, d), lambda i, j: (i, j, 0)),
            _resident((d, n_in)),
            _resident((LANES, key)),
            _resident((1, key)),
            _resident((1, val // heads)),
            _resident((val, d)),
            _resident((1, d)),
            _resident((1, d)),
        ],
        out_specs=pl.BlockSpec((None, tile, d), lambda i, j: (i, j, 0)),
        out_shape=jax.ShapeDtypeStruct((b, l, d), F32),
        scratch_shapes=[
            pltpu.VMEM((heads, val // heads, key // heads), F32),
            pltpu.VMEM((tile, val), F32),
        ],
        compiler_params=pltpu.CompilerParams(
            dimension_semantics=("parallel", "arbitrary"), vmem_limit_bytes=VMEM_LIMIT),
        name="gla_ln",
    )(h, w_in_p, w_gate_p, b_gate.reshape(1, key), norm_g.reshape(1, -1), w_out.astype(BF),
      ln_g.reshape(1, d), ln_b.reshape(1, d))


def _ret_body(x_ref, cos_ref, sin_ref, w_in_ref, ng_ref, w_out_ref, lg_ref, lb_ref,
              o_ref, st_ref, ov_ref, *, key, val, heads):
    dk = key // heads
    dv = val // heads
    half = dk // 2
    tile = x_ref.shape[0]

    @pl.when(pl.program_id(1) == 0)
    def _():
        st_ref[...] = jnp.zeros_like(st_ref)

    x = x_ref[...]
    xb = x.astype(BF)
    cos = cos_ref[...]
    sin = sin_ref[...]
    ri = lax.broadcasted_iota(jnp.int32, (tile, tile), 0)
    ci = lax.broadcasted_iota(jnp.int32, (tile, tile), 1)
    causal = ci <= ri
    dist = (ri - ci).astype(F32)
    pos = lax.broadcasted_iota(jnp.int32, (tile, 1), 0).astype(F32)
    ng = ng_ref[...]

    def rope(t):
        t1 = t[:, :half]
        t2 = t[:, half:]
        return jnp.concatenate([t1 * cos - t2 * sin, t1 * sin + t2 * cos], axis=1)

    for h in range(heads):
        log_g = math.log(1.0 - 2.0 ** (-5.0 - h))
        qh = rope(_dot(xb, w_in_ref[:, h * dk:(h + 1) * dk]))
        kh = rope(_dot(xb, w_in_ref[:, key + h * dk:key + (h + 1) * dk])) * (dk ** -0.5)
        vh = _dot(xb, w_in_ref[:, 2 * key + h * dv:2 * key + (h + 1) * dv]).astype(BF)
        qb = qh.astype(BF)
        decay = jnp.where(causal, jnp.exp(dist * log_g), 0.0)
        s = (_dot_nt(qb, kh.astype(BF)) * decay).astype(BF)
        st = st_ref[h]
        o = _dot(s, vh) + _dot(qb, st.astype(BF)) * jnp.exp((pos + 1.0) * log_g)
        k_dec = (kh * jnp.exp((tile - 1.0 - pos) * log_g)).astype(BF)
        st_ref[h] = math.exp(tile * log_g) * st + _dot_tn(k_dec, vh)
        mu = jnp.mean(o, axis=-1, keepdims=True)
        oc = o - mu
        var = jnp.mean(oc * oc, axis=-1, keepdims=True)
        ov_ref[:, h * dv:(h + 1) * dv] = oc * lax.rsqrt(var + NORM_EPS) * ng

    g = _dot(xb, w_in_ref[:, 2 * key + val:2 * key + 2 * val])
    y = (ov_ref[...] * _silu(g)).astype(BF)
    m = _dot(y, w_out_ref[...])
    o_ref[...] = _layer_norm(DN_ALPHA * x + m, lg_ref[...], lb_ref[...])


def _ret_layer(h, w_in, norm_g, w_out, ln_g, ln_b):
    b, l, d = h.shape
    val = w_out.shape[0]
    key = (w_in.shape[1] - 2 * val) // 2
    heads = RET_HEADS
    dk = key // heads
    tile = min(MIX_TILE, l)
    inv = ROPE_BASE ** (-jnp.arange(0, dk, 2, dtype=F32) / dk)
    ang = jnp.arange(l, dtype=F32)[:, None] * inv[None, :]
    cos = jnp.cos(ang)
    sin = jnp.sin(ang)
    body = functools.partial(_ret_body, key=key, val=val, heads=heads)
    return pl.pallas_call(
        body,
        grid=(b, l // tile),
        in_specs=[
            pl.BlockSpec((None, tile, d), lambda i, j: (i, j, 0)),
            pl.BlockSpec((tile, dk // 2), lambda i, j: (j, 0)),
            pl.BlockSpec((tile, dk // 2), lambda i, j: (j, 0)),
            _resident((d, w_in.shape[1])),
            _resident((1, val // heads)),
            _resident((val, d)),
            _resident((1, d)),
            _resident((1, d)),
        ],
        out_specs=pl.BlockSpec((None, tile, d), lambda i, j: (i, j, 0)),
        out_shape=jax.ShapeDtypeStruct((b, l, d), F32),
        scratch_shapes=[
            pltpu.VMEM((heads, dk, val // heads), F32),
            pltpu.VMEM((tile, val), F32),
        ],
        compiler_params=pltpu.CompilerParams(
            dimension_semantics=("parallel", "arbitrary"), vmem_limit_bytes=VMEM_LIMIT),
        name="ret_ln",
    )(h, cos, sin, w_in.astype(BF), norm_g.reshape(1, -1), w_out.astype(BF),
      ln_g.reshape(1, d), ln_b.reshape(1, d))


def _ssd_body(x_ref, w_in_ref, cw_ref, cb_ref, dtb_ref, alog_ref, dsk_ref, ng_ref, ex_ref,
              w_out_ref, lg_ref, lb_ref, o_ref, st_ref, cbuf_ref, y_ref, *, inner, groups, state):
    tile = x_ref.shape[0]
    gw = inner // groups
    pair = 2 * SSD_HEADDIM
    conv_dim = inner + 2 * groups * state

    @pl.when(pl.program_id(1) == 0)
    def _():
        st_ref[...] = jnp.zeros_like(st_ref)
        cbuf_ref[0:SUBLANES, :] = jnp.zeros((SUBLANES, conv_dim), F32)

    x = x_ref[...]
    xb = x.astype(BF)

    cbuf_ref[SUBLANES:SUBLANES + tile, :] = _dot(xb, w_in_ref[:, inner:inner + conv_dim])
    acc = cb_ref[...]
    for kk in range(SSD_CONV):
        back = SSD_CONV - 1 - kk
        acc = acc + cw_ref[kk:kk + 1, :] * cbuf_ref[SUBLANES - back:SUBLANES - back + tile, :]
    cbuf_ref[0:SUBLANES, :] = cbuf_ref[tile:tile + SUBLANES, :]
    xbc = _silu(acc)
    xs = xbc[:, :inner]
    bm = xbc[:, inner:inner + groups * state].astype(BF)
    cm = xbc[:, inner + groups * state:].astype(BF)

    dt_raw = _dot(xb, w_in_ref[:, inner + conv_dim:inner + conv_dim + LANES])
    dt = _softplus(_sel_dot_rhs(dt_raw, ex_ref[...], 3) + dtb_ref[...])
    da = dt * (-jnp.exp(alog_ref[...]))
    cum = _sel_dot_lhs(_chunk_tril(tile, CHUNK), da, 3)
    xdt = xs * dt

    ri = lax.broadcasted_iota(jnp.int32, (CHUNK, pair), 0)
    ci = lax.broadcasted_iota(jnp.int32, (CHUNK, pair), 1)
    cj = ci % CHUNK
    causal = cj <= ri
    diag = cj == ri
    lane = lax.broadcasted_iota(jnp.int32, (CHUNK, pair), 1)
    first = lane < SSD_HEADDIM
    dsk = dsk_ref[...]

    for c in range(tile // CHUNK):
        rows = slice(c * CHUNK, (c + 1) * CHUNK)
        cum_c = cum[rows]
        last = cum_c[CHUNK - 1:CHUNK, :]
        e_cum = jnp.exp(cum_c)
        x_end = (xdt[rows] * jnp.exp(last - cum_c)).astype(BF)
        dec = jnp.exp(last)
        for g in range(groups):
            sl = slice(g * state, (g + 1) * state)
            gl = slice(g * gw, (g + 1) * gw)
            cg = cm[rows, sl]
            bg = bm[rows, sl]
            cb2 = _dot_nt(cg, jnp.concatenate([bg, bg], axis=0))
            st = st_ref[g]
            y_state = _dot(cg, st.astype(BF)) * e_cum[:, gl]
            st_ref[g] = st * dec[:, gl] + _dot_tn(bg, x_end[:, gl])
            ys = []
            for p in range(gw // pair):
                pl_ = slice(g * gw + p * pair, g * gw + (p + 1) * pair)
                col = cum_c[:, pl_]
                row = jnp.sum(jnp.where(diag, col, 0.0), axis=0, keepdims=True)
                seg = jnp.exp(jnp.where(causal, col - row, -1e30))
                w = (seg * cb2).astype(BF)
                xp = xdt[rows, pl_].astype(BF)
                zero = jnp.zeros_like(xp)
                rhs = jnp.concatenate([jnp.where(first, xp, zero), jnp.where(first, zero, xp)], axis=0)
                ys.append(_dot(w, rhs))
            y_ref[rows, gl] = jnp.concatenate(ys, axis=1) + y_state + xs[rows, gl] * dsk[:, gl]

    z = _dot(xb, w_in_ref[:, 0:inner])
    ng = ng_ref[...]
    normed = []
    for g in range(groups):
        gl = slice(g * gw, (g + 1) * gw)
        yg = y_ref[:, gl] * _silu(z[:, gl])
        ms = jnp.mean(yg * yg, axis=-1, keepdims=True)
        normed.append((yg * lax.rsqrt(ms + NORM_EPS) * ng[:, gl]).astype(BF))
    m = _dot(jnp.concatenate(normed, axis=1), w_out_ref[...])
    o_ref[...] = _layer_norm(DN_ALPHA * x + m, lg_ref[...], lb_ref[...])


def _ssd_layer(h, w_in, conv_w, conv_b, dt_bias, a_log, d_skip, norm_g, w_out, ln_g, ln_b):
    b, l, d = h.shape
    inner = w_out.shape[0]
    heads = dt_bias.shape[0]
    groups = SSD_GROUPS
    state = SSD_STATE
    conv_dim = conv_w.shape[1]
    assert inner // heads == SSD_HEADDIM and conv_dim == inner + 2 * groups * state
    tile = min(MIX_TILE, l)
    w_in_p = jnp.pad(w_in, ((0, 0), (0, LANES - heads))).astype(BF)
    expand = (jnp.arange(LANES)[:, None] == (jnp.arange(inner) // SSD_HEADDIM)[None, :]).astype(BF)
    rep = lambda t: jnp.repeat(t, SSD_HEADDIM).reshape(1, inner)
    body = functools.partial(_ssd_body, inner=inner, groups=groups, state=state)
    return pl.pallas_call(
        body,
        grid=(b, l // tile),
        in_specs=[
            pl.BlockSpec((None, tile, d), lambda i, j: (i, j, 0)),
            _resident((d, w_in_p.shape[1])),
            _resident((SSD_CONV, conv_dim)),
            _resident((1, conv_dim)),
            _resident((1, inner)),
            _resident((1, inner)),
            _resident((1, inner)),
            _resident((1, inner)),
            _resident((LANES, inner)),
            _resident((inner, d)),
            _resident((1, d)),
            _resident((1, d)),
        ],
        out_specs=pl.BlockSpec((None, tile, d), lambda i, j: (i, j, 0)),
        out_shape=jax.ShapeDtypeStruct((b, l, d), F32),
        scratch_shapes=[
            pltpu.VMEM((groups, state, inner // groups), F32),
            pltpu.VMEM((SUBLANES + tile, conv_dim), F32),
            pltpu.VMEM((tile, inner), F32),
        ],
        compiler_params=pltpu.CompilerParams(
            dimension_semantics=("parallel", "arbitrary"), vmem_limit_bytes=VMEM_LIMIT),
        name="ssd_ln",
    )(h, w_in_p, conv_w, conv_b.reshape(1, conv_dim), rep(dt_bias), rep(a_log), rep(d_skip),
      norm_g.reshape(1, inner), expand, w_out.astype(BF), ln_g.reshape(1, d), ln_b.reshape(1, d))


def kernel(x, ffn_w_in, ffn_w_out, ln_g, ln_b, gla_w_in, gla_w_gate, gla_b_gate, gla_norm_g, gla_w_out,
           ret_w_in, ret_norm_g, ret_w_out, ssd_w_in, ssd_conv_w, ssd_conv_b, ssd_dt_bias, ssd_a_log,
           ssd_d, ssd_norm_g, ssd_w_out):
    b, l, d = x.shape
    h = x
    for i in range(DEPTH):
        h = _ffn_layer(h.reshape(b * l, d), ffn_w_in[i, 0], ffn_w_out[i, 0], ln_g[i, 0], ln_b[i, 0])
        h = h.reshape(b, l, d)
        kind, j = i % N_MIXERS, i // N_MIXERS
        if kind == 0:
            h = _gla_layer(h, gla_w_in[j], gla_w_gate[j], gla_b_gate[j], gla_norm_g[j], gla_w_out[j],
                           ln_g[i, 1], ln_b[i, 1])
        elif kind == 1:
            h = _ret_layer(h, ret_w_in[j], ret_norm_g[j], ret_w_out[j], ln_g[i, 1], ln_b[i, 1])
        else:
            h = _ssd_layer(h, ssd_w_in[j], ssd_conv_w[j], ssd_conv_b[j], ssd_dt_bias[j], ssd_a_log[j],
                           ssd_d[j], ssd_norm_g[j], ssd_w_out[j], ln_g[i, 1], ln_b[i, 1])
        h = _ffn_layer(h.reshape(b * l, d), ffn_w_in[i, 1], ffn_w_out[i, 1], ln_g[i, 2], ln_b[i, 2])
        h = h.reshape(b, l, d)
    return h
```

```python
import functools
import math

import jax
import jax.numpy as jnp
from jax import lax
from jax.experimental import pallas as pl
from jax.experimental.pallas import tpu as pltpu

BF = jnp.bfloat16
F32 = jnp.float32

LANES = 128
SUBLANES = 8
VMEM_LIMIT = 56 * 1024 * 1024

DEPTH = 4
N_MIXERS = 3
CHUNK = 64
LN_EPS = 1e-5
NORM_EPS = 1e-6
DN_ALPHA = (2 * DEPTH) ** 0.25

GLA_HEADS = 4
GLA_TAU = 16.0
RET_HEADS = 4
ROPE_BASE = 10000.0
SSD_HEADDIM = 64
SSD_GROUPS = 4
SSD_STATE = 128
SSD_CONV = 4

FFN_TILE = 1024
FFN_SUB = 256
GLA_TILE = 512
MIX_TILE = 256
MIX_SUB = 256


def _dot(a, b):
    return jnp.dot(a, b, preferred_element_type=F32)


def _dot_nt(a, b):
    return lax.dot_general(a, b, (((1,), (1,)), ((), ())), preferred_element_type=F32)


def _dot_tn(a, b):
    return lax.dot_general(a, b, (((0,), (0,)), ((), ())), preferred_element_type=F32)


def _split_bf16(x, parts):
    out = []
    r = x
    for i in range(parts):
        p = r.astype(BF)
        out.append(p)
        if i + 1 < parts:
            r = r - p.astype(F32)
    return out


def _sel_dot_lhs(sel, x, parts):
    acc = None
    for p in _split_bf16(x, parts):
        t = _dot(sel, p)
        acc = t if acc is None else acc + t
    return acc


def _sel_dot_rhs(x, sel, parts):
    acc = None
    for p in _split_bf16(x, parts):
        t = _dot(p, sel)
        acc = t if acc is None else acc + t
    return acc


def _sigmoid(x):
    return 1.0 / (1.0 + jnp.exp(-x))


def _silu(x):
    return x * _sigmoid(x)


def _softplus(x):
    return jnp.maximum(x, 0.0) + jnp.log(1.0 + jnp.exp(-jnp.abs(x)))


def _layer_norm(z, g, b):
    mu = jnp.mean(z, axis=-1, keepdims=True)
    zc = z - mu
    var = jnp.mean(zc * zc, axis=-1, keepdims=True)
    return zc * lax.rsqrt(var + LN_EPS) * g + b


def _chunk_causal(tile, chunk):
    ri = lax.broadcasted_iota(jnp.int32, (tile, tile), 0)
    ci = lax.broadcasted_iota(jnp.int32, (tile, tile), 1)
    return ((ri // chunk) == (ci // chunk)) & (ci <= ri)


def _chunk_tril(tile, chunk):
    return jnp.where(_chunk_causal(tile, chunk), 1.0, 0.0).astype(BF)


def _resident(shape):
    nd = len(shape)
    return pl.BlockSpec(shape, lambda *_: (0,) * nd, pipeline_mode=pl.Buffered(1))


def _ffn_body(x_ref, w_in_ref, w_out_ref, g_ref, b_ref, o_ref, *, d_ff, sub):
    for s in range(x_ref.shape[0] // sub):
        rows = slice(s * sub, (s + 1) * sub)
        x = x_ref[rows, :]
        xb = x.astype(BF)
        gate = _dot(xb, w_in_ref[:, :d_ff])
        up = _dot(xb, w_in_ref[:, d_ff:])
        act = (_silu(gate) * up).astype(BF)
        y = _dot(act, w_out_ref[...])
        o_ref[rows, :] = _layer_norm(DN_ALPHA * x + 0.5 * y, g_ref[...], b_ref[...])


def _ffn_layer(h, w_in, w_out, g, b):
    t, d = h.shape
    d_ff = w_out.shape[0]
    tm = min(FFN_TILE, t)
    return pl.pallas_call(
        functools.partial(_ffn_body, d_ff=d_ff, sub=min(FFN_SUB, tm)),
        grid=(t // tm,),
        in_specs=[
            pl.BlockSpec((tm, d), lambda i: (i, 0)),
            _resident((d, 2 * d_ff)),
            _resident((d_ff, d)),
            _resident((1, d)),
            _resident((1, d)),
        ],
        out_specs=pl.BlockSpec((tm, d), lambda i: (i, 0)),
        out_shape=jax.ShapeDtypeStruct((t, d), F32),
        compiler_params=pltpu.CompilerParams(
            dimension_semantics=("parallel",), vmem_limit_bytes=VMEM_LIMIT),
        name="ffn_ln",
    )(h, w_in.astype(BF), w_out.astype(BF), g.reshape(1, d), b.reshape(1, d))


def _gla_body(x_ref, w_in_ref, w_gate_ref, b_gate_ref, ng_ref, w_out_ref, lg_ref, lb_ref,
              o_ref, st_ref, ov_ref, *, key, val, heads, sub):
    @pl.when(pl.program_id(1) == 0)
    def _():
        st_ref[...] = jnp.zeros_like(st_ref)

    for s in range(x_ref.shape[0] // sub):
        rows = slice(s * sub, (s + 1) * sub)
        _gla_rows(x_ref.at[rows, :], w_in_ref, w_gate_ref, b_gate_ref, ng_ref, w_out_ref, lg_ref, lb_ref,
                  o_ref.at[rows, :], st_ref, ov_ref.at[rows, :], key=key, val=val, heads=heads)


def _gla_rows(x_ref, w_in_ref, w_gate_ref, b_gate_ref, ng_ref, w_out_ref, lg_ref, lb_ref,
              o_ref, st_ref, ov_ref, *, key, val, heads):
    dk = key // heads
    dv = val // heads
    tile = x_ref.shape[0]

    x = x_ref[...]
    xb = x.astype(BF)
    g_low = _dot(xb, w_in_ref[:, 2 * key + 2 * val:2 * key + 2 * val + LANES]).astype(BF)
    zg = _dot(g_low, w_gate_ref[...]) + b_gate_ref[...]
    log_a = (jnp.minimum(zg, 0.0) - jnp.log(1.0 + jnp.exp(-jnp.abs(zg)))) * (1.0 / GLA_TAU)
    causal = _chunk_causal(tile, CHUNK)
    cum = _sel_dot_lhs(jnp.where(causal, 1.0, 0.0).astype(BF), log_a, 3)
    q = _dot(xb, w_in_ref[:, 0:key]) * (dk ** -0.5)
    k = _dot(xb, w_in_ref[:, key:2 * key])
    v = _dot(xb, w_in_ref[:, 2 * key:2 * key + val]).astype(BF)

    n_chunks = tile // CHUNK
    tots = [cum[(c + 1) * CHUNK - 1:(c + 1) * CHUNK, :] for c in range(n_chunks)]
    tot_rows = jnp.concatenate([jnp.broadcast_to(t, (CHUNK, key)) for t in tots], axis=0)
    q_f = q * jnp.exp(cum)
    q_dec = q_f.astype(BF)
    k_inv = (k * jnp.exp(-cum)).astype(BF)
    k_f = k * jnp.exp(tot_rows - cum)

    for h in range(heads):
        kl = slice(h * dk, (h + 1) * dk)
        vl = slice(h * dv, (h + 1) * dv)
        s = jnp.where(causal, _dot_nt(q_dec[:, kl], k_inv[:, kl]), 0.0).astype(BF)
        ov_ref[:, vl] = _dot(s, v[:, vl])

    head_of_lane = lax.broadcasted_iota(jnp.int32, (CHUNK, key), 1) // dk
    st = st_ref[...]
    for c in range(n_chunks):
        rows = slice(c * CHUNK, (c + 1) * CHUNK)
        qc = q_f[rows]
        kc = k_f[rows]
        qs = jnp.concatenate([jnp.where(head_of_lane == h, qc, 0.0) for h in range(heads)], axis=0).astype(BF)
        ks = jnp.concatenate([jnp.where(head_of_lane == h, kc, 0.0) for h in range(heads)], axis=0).astype(BF)
        vs = jnp.concatenate([v[rows, h * dv:(h + 1) * dv] for h in range(heads)], axis=0)
        o_int = _dot_nt(qs, st.astype(BF))
        for h in range(heads):
            ov_ref[rows, h * dv:(h + 1) * dv] += o_int[h * CHUNK:(h + 1) * CHUNK, :]
        st = st * jnp.exp(tots[c]) + _dot_tn(vs, ks)
    st_ref[...] = st

    r = _dot(xb, w_in_ref[:, 2 * key + val:2 * key + 2 * val])
    ng = ng_ref[...]
    normed = []
    for h in range(heads):
        oh = ov_ref[:, h * dv:(h + 1) * dv]
        ms = jnp.mean(oh * oh, axis=-1, keepdims=True)
        normed.append(oh * lax.rsqrt(ms + NORM_EPS) * ng)
    y = (jnp.concatenate(normed, axis=1) * _silu(r)).astype(BF)
    m = _dot(y, w_out_ref[...])
    o_ref[...] = _layer_norm(DN_ALPHA * x + m, lg_ref[...], lb_ref[...])


def _gla_layer(h, w_in, w_gate, b_gate, norm_g, w_out, ln_g, ln_b):
    b, l, d = h.shape
    rank, key = w_gate.shape
    val = w_out.shape[0]
    heads = GLA_HEADS
    tile = min(GLA_TILE, l)
    w_in_p = jnp.pad(w_in, ((0, 0), (0, LANES - rank))).astype(BF)
    w_gate_p = jnp.pad(w_gate, ((0, LANES - rank), (0, 0))).astype(BF)
    n_in = w_in_p.shape[1]
    body = functools.partial(_gla_body, key=key, val=val, heads=heads, sub=min(MIX_SUB, tile))
    return pl.pallas_call(
        body,
        grid=(b, l // tile),
        in_specs=[
            pl.BlockSpec((None, tile, d), lambda i, j: (i, j, 0)),
            _resident((d, n_in)),
            _resident((LANES, key)),
            _resident((1, key)),
            _resident((1, val // heads)),
            _resident((val, d)),
            _resident((1, d)),
            _resident((1, d)),
        ],
        out_specs=pl.BlockSpec((None, tile, d), lambda i, j: (i, j, 0)),
        out_shape=jax.ShapeDtypeStruct((b, l, d), F32),
        scratch_shapes=[
            pltpu.VMEM((val // heads, key), F32),
            pltpu.VMEM((tile, val), F32),
        ],
        compiler_params=pltpu.CompilerParams(
            dimension_semantics=("parallel", "arbitrary"), vmem_limit_bytes=VMEM_LIMIT),
        name="gla_ln",
    )(h, w_in_p, w_gate_p, b_gate.reshape(1, key), norm_g.reshape(1, -1), w_out.astype(BF),
      ln_g.reshape(1, d), ln_b.reshape(1, d))


def _ret_body(x_ref, cos_ref, sin_ref, w_in_ref, ng_ref, w_out_ref, lg_ref, lb_ref,
              o_ref, st_ref, ov_ref, *, key, val, heads):
    dk = key // heads
    dv = val // heads
    half = dk // 2
    tile = x_ref.shape[0]

    @pl.when(pl.program_id(1) == 0)
    def _():
        st_ref[...] = jnp.zeros_like(st_ref)

    x = x_ref[...]
    xb = x.astype(BF)
    cos = cos_ref[...]
    sin = sin_ref[...]
    ri = lax.broadcasted_iota(jnp.int32, (tile, tile), 0)
    ci = lax.broadcasted_iota(jnp.int32, (tile, tile), 1)
    causal = ci <= ri
    dist = (ri - ci).astype(F32)
    pos = lax.broadcasted_iota(jnp.int32, (tile, 1), 0).astype(F32)
    ng = ng_ref[...]

    def rope(t):
        t1 = t[:, :half]
        t2 = t[:, half:]
        return jnp.concatenate([t1 * cos - t2 * sin, t1 * sin + t2 * cos], axis=1)

    for h in range(heads):
        log_g = math.log(1.0 - 2.0 ** (-5.0 - h))
        qh = rope(_dot(xb, w_in_ref[:, h * dk:(h + 1) * dk]))
        kh = rope(_dot(xb, w_in_ref[:, key + h * dk:key + (h + 1) * dk])) * (dk ** -0.5)
        vh = _dot(xb, w_in_ref[:, 2 * key + h * dv:2 * key + (h + 1) * dv]).astype(BF)
        qb = qh.astype(BF)
        decay = jnp.where(causal, jnp.exp(dist * log_g), 0.0)
        s = (_dot_nt(qb, kh.astype(BF)) * decay).astype(BF)
        st = st_ref[h]
        o = _dot(s, vh) + _dot(qb, st.astype(BF)) * jnp.exp((pos + 1.0) * log_g)
        k_dec = (kh * jnp.exp((tile - 1.0 - pos) * log_g)).astype(BF)
        st_ref[h] = math.exp(tile * log_g) * st + _dot_tn(k_dec, vh)
        mu = jnp.mean(o, axis=-1, keepdims=True)
        oc = o - mu
        var = jnp.mean(oc * oc, axis=-1, keepdims=True)
        ov_ref[:, h * dv:(h + 1) * dv] = oc * lax.rsqrt(var + NORM_EPS) * ng

    g = _dot(xb, w_in_ref[:, 2 * key + val:2 * key + 2 * val])
    y = (ov_ref[...] * _silu(g)).astype(BF)
    m = _dot(y, w_out_ref[...])
    o_ref[...] = _layer_norm(DN_ALPHA * x + m, lg_ref[...], lb_ref[...])


def _ret_layer(h, w_in, norm_g, w_out, ln_g, ln_b):
    b, l, d = h.shape
    val = w_out.shape[0]
    key = (w_in.shape[1] - 2 * val) // 2
    heads = RET_HEADS
    dk = key // heads
    tile = min(MIX_TILE, l)
    inv = ROPE_BASE ** (-jnp.arange(0, dk, 2, dtype=F32) / dk)
    ang = jnp.arange(l, dtype=F32)[:, None] * inv[None, :]
    cos = jnp.cos(ang)
    sin = jnp.sin(ang)
    body = functools.partial(_ret_body, key=key, val=val, heads=heads)
    return pl.pallas_call(
        body,
        grid=(b, l // tile),
        in_specs=[
            pl.BlockSpec((None, tile, d), lambda i, j: (i, j, 0)),
            pl.BlockSpec((tile, dk // 2), lambda i, j: (j, 0)),
            pl.BlockSpec((tile, dk // 2), lambda i, j: (j, 0)),
            _resident((d, w_in.shape[1])),
            _resident((1, val // heads)),
            _resident((val, d)),
            _resident((1, d)),
            _resident((1, d)),
        ],
        out_specs=pl.BlockSpec((None, tile, d), lambda i, j: (i, j, 0)),
        out_shape=jax.ShapeDtypeStruct((b, l, d), F32),
        scratch_shapes=[
            pltpu.VMEM((heads, dk, val // heads), F32),
            pltpu.VMEM((tile, val), F32),
        ],
        compiler_params=pltpu.CompilerParams(
            dimension_semantics=("parallel", "arbitrary"), vmem_limit_bytes=VMEM_LIMIT),
        name="ret_ln",
    )(h, cos, sin, w_in.astype(BF), norm_g.reshape(1, -1), w_out.astype(BF),
      ln_g.reshape(1, d), ln_b.reshape(1, d))


def _ssd_body(x_ref, w_in_ref, cw_ref, cb_ref, dtb_ref, alog_ref, dsk_ref, ng_ref, ex_ref,
              w_out_ref, lg_ref, lb_ref, o_ref, st_ref, cbuf_ref, y_ref, *, inner, groups, state):
    tile = x_ref.shape[0]
    gw = inner // groups
    pair = 2 * SSD_HEADDIM
    conv_dim = inner + 2 * groups * state

    @pl.when(pl.program_id(1) == 0)
    def _():
        st_ref[...] = jnp.zeros_like(st_ref)
        cbuf_ref[0:SUBLANES, :] = jnp.zeros((SUBLANES, conv_dim), F32)

    x = x_ref[...]
    xb = x.astype(BF)

    cbuf_ref[SUBLANES:SUBLANES + tile, :] = _dot(xb, w_in_ref[:, inner:inner + conv_dim])
    acc = cb_ref[...]
    for kk in range(SSD_CONV):
        back = SSD_CONV - 1 - kk
        acc = acc + cw_ref[kk:kk + 1, :] * cbuf_ref[SUBLANES - back:SUBLANES - back + tile, :]
    cbuf_ref[0:SUBLANES, :] = cbuf_ref[tile:tile + SUBLANES, :]
    xbc = _silu(acc)
    xs = xbc[:, :inner]
    bm = xbc[:, inner:inner + groups * state].astype(BF)
    cm = xbc[:, inner + groups * state:].astype(BF)

    dt_raw = _dot(xb, w_in_ref[:, inner + conv_dim:inner + conv_dim + LANES])
    dt_h = _softplus(dt_raw + dtb_ref[...])
    da_h = dt_h * (-jnp.exp(alog_ref[...]))
    cum_h = _sel_dot_lhs(_chunk_tril(tile, CHUNK), da_h, 3)
    dt = _sel_dot_rhs(dt_h, ex_ref[...], 2)
    cum = _sel_dot_rhs(cum_h, ex_ref[...], 3)
    xdt = xs * dt

    ri = lax.broadcasted_iota(jnp.int32, (CHUNK, pair), 0)
    ci = lax.broadcasted_iota(jnp.int32, (CHUNK, pair), 1)
    cj = ci % CHUNK
    causal = cj <= ri
    diag = cj == ri
    lane = lax.broadcasted_iota(jnp.int32, (CHUNK, pair), 1)
    first = lane < SSD_HEADDIM
    dsk = dsk_ref[...]

    for c in range(tile // CHUNK):
        rows = slice(c * CHUNK, (c + 1) * CHUNK)
        cum_c = cum[rows]
        last = cum_c[CHUNK - 1:CHUNK, :]
        e_cum = jnp.exp(cum_c)
        x_end = (xdt[rows] * jnp.exp(last - cum_c)).astype(BF)
        dec = jnp.exp(last)
        for g in range(groups):
            sl = slice(g * state, (g + 1) * state)
            gl = slice(g * gw, (g + 1) * gw)
            cg = cm[rows, sl]
            bg = bm[rows, sl]
            cb2 = _dot_nt(cg, jnp.concatenate([bg, bg], axis=0))
            st = st_ref[g]
            y_state = _dot(cg, st.astype(BF)) * e_cum[:, gl]
            st_ref[g] = st * dec[:, gl] + _dot_tn(bg, x_end[:, gl])
            ys = []
            for p in range(gw // pair):
                pl_ = slice(g * gw + p * pair, g * gw + (p + 1) * pair)
                col = cum_c[:, pl_]
                row = jnp.sum(jnp.where(diag, col, 0.0), axis=0, keepdims=True)
                seg = jnp.exp(jnp.where(causal, col - row, -1e30))
                w = (seg * cb2).astype(BF)
                xp = xdt[rows, pl_].astype(BF)
                zero = jnp.zeros_like(xp)
                rhs = jnp.concatenate([jnp.where(first, xp, zero), jnp.where(first, zero, xp)], axis=0)
                ys.append(_dot(w, rhs))
            y_ref[rows, gl] = jnp.concatenate(ys, axis=1) + y_state + xs[rows, gl] * dsk[:, gl]

    z = _dot(xb, w_in_ref[:, 0:inner])
    ng = ng_ref[...]
    normed = []
    for g in range(groups):
        gl = slice(g * gw, (g + 1) * gw)
        yg = y_ref[:, gl] * _silu(z[:, gl])
        ms = jnp.mean(yg * yg, axis=-1, keepdims=True)
        normed.append((yg * lax.rsqrt(ms + NORM_EPS) * ng[:, gl]).astype(BF))
    m = _dot(jnp.concatenate(normed, axis=1), w_out_ref[...])
    o_ref[...] = _layer_norm(DN_ALPHA * x + m, lg_ref[...], lb_ref[...])


def _ssd_layer(h, w_in, conv_w, conv_b, dt_bias, a_log, d_skip, norm_g, w_out, ln_g, ln_b):
    b, l, d = h.shape
    inner = w_out.shape[0]
    heads = dt_bias.shape[0]
    groups = SSD_GROUPS
    state = SSD_STATE
    conv_dim = conv_w.shape[1]
    assert inner // heads == SSD_HEADDIM and conv_dim == inner + 2 * groups * state
    tile = min(MIX_TILE, l)
    w_in_p = jnp.pad(w_in, ((0, 0), (0, LANES - heads))).astype(BF)
    expand = (jnp.arange(LANES)[:, None] == (jnp.arange(inner) // SSD_HEADDIM)[None, :]).astype(BF)
    rep = lambda t: jnp.repeat(t, SSD_HEADDIM).reshape(1, inner)
    lane_pad = lambda t: jnp.pad(t, (0, LANES - heads)).reshape(1, LANES)
    body = functools.partial(_ssd_body, inner=inner, groups=groups, state=state)
    return pl.pallas_call(
        body,
        grid=(b, l // tile),
        in_specs=[
            pl.BlockSpec((None, tile, d), lambda i, j: (i, j, 0)),
            _resident((d, w_in_p.shape[1])),
            _resident((SSD_CONV, conv_dim)),
            _resident((1, conv_dim)),
            _resident((1, LANES)),
            _resident((1, LANES)),
            _resident((1, inner)),
            _resident((1, inner)),
            _resident((LANES, inner)),
            _resident((inner, d)),
            _resident((1, d)),
            _resident((1, d)),
        ],
        out_specs=pl.BlockSpec((None, tile, d), lambda i, j: (i, j, 0)),
        out_shape=jax.ShapeDtypeStruct((b, l, d), F32),
        scratch_shapes=[
            pltpu.VMEM((groups, state, inner // groups), F32),
            pltpu.VMEM((SUBLANES + tile, conv_dim), F32),
            pltpu.VMEM((tile, inner), F32),
        ],
        compiler_params=pltpu.CompilerParams(
            dimension_semantics=("parallel", "arbitrary"), vmem_limit_bytes=VMEM_LIMIT),
        name="ssd_ln",
    )(h, w_in_p, conv_w, conv_b.reshape(1, conv_dim), lane_pad(dt_bias), lane_pad(a_log), rep(d_skip),
      norm_g.reshape(1, inner), expand, w_out.astype(BF), ln_g.reshape(1, d), ln_b.reshape(1, d))


def kernel(x, ffn_w_in, ffn_w_out, ln_g, ln_b, gla_w_in, gla_w_gate, gla_b_gate, gla_norm_g, gla_w_out,
           ret_w_in, ret_norm_g, ret_w_out, ssd_w_in, ssd_conv_w, ssd_conv_b, ssd_dt_bias, ssd_a_log,
           ssd_d, ssd_norm_g, ssd_w_out):
    b, l, d = x.shape
    h = x
    for i in range(DEPTH):
        h = _ffn_layer(h.reshape(b * l, d), ffn_w_in[i, 0], ffn_w_out[i, 0], ln_g[i, 0], ln_b[i, 0])
        h = h.reshape(b, l, d)
        kind, j = i % N_MIXERS, i // N_MIXERS
        if kind == 0:
            h = _gla_layer(h, gla_w_in[j], gla_w_gate[j], gla_b_gate[j], gla_norm_g[j], gla_w_out[j],
                           ln_g[i, 1], ln_b[i, 1])
        elif kind == 1:
            h = _ret_layer(h, ret_w_in[j], ret_norm_g[j], ret_w_out[j], ln_g[i, 1], ln_b[i, 1])
        else:
            h = _ssd_layer(h, ssd_w_in[j], ssd_conv_w[j], ssd_conv_b[j], ssd_dt_bias[j], ssd_a_log[j],
                           ssd_d[j], ssd_norm_g[j], ssd_w_out[j], ln_g[i, 1], ln_b[i, 1])
        h = _ffn_layer(h.reshape(b * l, d), ffn_w_in[i, 1], ffn_w_out[i, 1], ln_g[i, 2], ln_b[i, 2])
        h = h.reshape(b, l, d)
    return h
```

```python
import functools
import math

import jax
import jax.numpy as jnp
from jax import lax
from jax.experimental import pallas as pl
from jax.experimental.pallas import tpu as pltpu

BF = jnp.bfloat16
F32 = jnp.float32

LANES = 128
SUBLANES = 8
VMEM_LIMIT = 56 * 1024 * 1024

DEPTH = 4
N_MIXERS = 3
CHUNK = 64
LN_EPS = 1e-5
NORM_EPS = 1e-6
DN_ALPHA = (2 * DEPTH) ** 0.25

GLA_HEADS = 4
GLA_TAU = 16.0
RET_HEADS = 4
ROPE_BASE = 10000.0
SSD_HEADDIM = 64
SSD_GROUPS = 4
SSD_STATE = 128
SSD_CONV = 4

FFN_TILE = 1024
FFN_SUB = 256
GLA_TILE = 512
RET_TILE = 512
SSD_TILE = 256
MIX_SUB = 256


def _dot(a, b):
    return jnp.dot(a, b, preferred_element_type=F32)


def _dot_nt(a, b):
    return lax.dot_general(a, b, (((1,), (1,)), ((), ())), preferred_element_type=F32)


def _dot_tn(a, b):
    return lax.dot_general(a, b, (((0,), (0,)), ((), ())), preferred_element_type=F32)


def _split_bf16(x, parts):
    out = []
    r = x
    for i in range(parts):
        p = r.astype(BF)
        out.append(p)
        if i + 1 < parts:
            r = r - p.astype(F32)
    return out


def _sel_dot_lhs(sel, x, parts):
    acc = None
    for p in _split_bf16(x, parts):
        t = _dot(sel, p)
        acc = t if acc is None else acc + t
    return acc


def _sel_dot_rhs(x, sel, parts):
    acc = None
    for p in _split_bf16(x, parts):
        t = _dot(p, sel)
        acc = t if acc is None else acc + t
    return acc


def _lane_group_pieces(x, group):
    grp = lax.broadcasted_iota(jnp.int32, x.shape, 1) // group
    p0 = x.astype(BF).astype(F32)
    r1 = x - p0
    p1 = r1.astype(BF).astype(F32)
    r2 = r1 - p1
    return jnp.where(grp == 0, p0, jnp.where(grp == 1, p1, jnp.where(grp == 2, r2, 0.0))).astype(BF)


def _sigmoid(x):
    return 1.0 / (1.0 + jnp.exp(-x))


def _silu(x):
    return x * _sigmoid(x)


def _softplus(x):
    return jnp.maximum(x, 0.0) + jnp.log(1.0 + jnp.exp(-jnp.abs(x)))


def _layer_norm(z, g, b):
    mu = jnp.mean(z, axis=-1, keepdims=True)
    zc = z - mu
    var = jnp.mean(zc * zc, axis=-1, keepdims=True)
    return zc * lax.rsqrt(var + LN_EPS) * g + b


def _chunk_causal(tile, chunk):
    ri = lax.broadcasted_iota(jnp.int32, (tile, tile), 0)
    ci = lax.broadcasted_iota(jnp.int32, (tile, tile), 1)
    return ((ri // chunk) == (ci // chunk)) & (ci <= ri)


def _chunk_tril(tile, chunk):
    return jnp.where(_chunk_causal(tile, chunk), 1.0, 0.0).astype(BF)


def _resident(shape):
    nd = len(shape)
    return pl.BlockSpec(shape, lambda *_: (0,) * nd, pipeline_mode=pl.Buffered(1))


def _ffn_body(x_ref, w_in_ref, w_out_ref, g_ref, b_ref, o_ref, *, d_ff, sub):
    for s in range(x_ref.shape[0] // sub):
        rows = slice(s * sub, (s + 1) * sub)
        x = x_ref[rows, :]
        gu = _dot(x.astype(BF), w_in_ref[...])
        act = (_silu(gu[:, :d_ff]) * gu[:, d_ff:]).astype(BF)
        y = _dot(act, w_out_ref[...])
        o_ref[rows, :] = _layer_norm(DN_ALPHA * x + 0.5 * y, g_ref[...], b_ref[...])


def _ffn_layer(h, w_in, w_out, g, b):
    t, d = h.shape
    d_ff = w_out.shape[0]
    tm = min(FFN_TILE, t)
    return pl.pallas_call(
        functools.partial(_ffn_body, d_ff=d_ff, sub=min(FFN_SUB, tm)),
        grid=(t // tm,),
        in_specs=[
            pl.BlockSpec((tm, d), lambda i: (i, 0)),
            _resident((d, 2 * d_ff)),
            _resident((d_ff, d)),
            _resident((1, d)),
            _resident((1, d)),
        ],
        out_specs=pl.BlockSpec((tm, d), lambda i: (i, 0)),
        out_shape=jax.ShapeDtypeStruct((t, d), F32),
        compiler_params=pltpu.CompilerParams(
            dimension_semantics=("parallel",), vmem_limit_bytes=VMEM_LIMIT),
        name="ffn_ln",
    )(h, w_in.astype(BF), w_out.astype(BF), g.reshape(1, d), b.reshape(1, d))


def _gla_body(x_ref, w_in_ref, w_gate_ref, b_gate_ref, ng_ref, w_out_ref, lg_ref, lb_ref,
              o_ref, st_ref, ov_ref, *, key, val, heads, sub):
    @pl.when(pl.program_id(1) == 0)
    def _():
        st_ref[...] = jnp.zeros_like(st_ref)

    for s in range(x_ref.shape[0] // sub):
        rows = slice(s * sub, (s + 1) * sub)
        _gla_rows(x_ref.at[rows, :], w_in_ref, w_gate_ref, b_gate_ref, ng_ref, w_out_ref, lg_ref, lb_ref,
                  o_ref.at[rows, :], st_ref, ov_ref.at[rows, :], key=key, val=val, heads=heads)


def _gla_rows(x_ref, w_in_ref, w_gate_ref, b_gate_ref, ng_ref, w_out_ref, lg_ref, lb_ref,
              o_ref, st_ref, ov_ref, *, key, val, heads):
    dk = key // heads
    dv = val // heads
    tile = x_ref.shape[0]

    x = x_ref[...]
    xb = x.astype(BF)
    g_low = _dot(xb, w_in_ref[:, 2 * key + 2 * val:2 * key + 2 * val + LANES]).astype(BF)
    q = _dot(xb, w_in_ref[:, 0:key]) * (dk ** -0.5)
    zg = _dot(g_low, w_gate_ref[...]) + b_gate_ref[...]
    k = _dot(xb, w_in_ref[:, key:2 * key])
    log_a = (jnp.minimum(zg, 0.0) - jnp.log(1.0 + jnp.exp(-jnp.abs(zg)))) * (1.0 / GLA_TAU)
    causal = _chunk_causal(tile, CHUNK)
    cum = _sel_dot_lhs(jnp.where(causal, 1.0, 0.0).astype(BF), log_a, 3)
    v = _dot(xb, w_in_ref[:, 2 * key:2 * key + val]).astype(BF)

    n_chunks = tile // CHUNK
    tots = [cum[(c + 1) * CHUNK - 1:(c + 1) * CHUNK, :] for c in range(n_chunks)]
    tot_rows = jnp.concatenate([jnp.broadcast_to(t, (CHUNK, key)) for t in tots], axis=0)
    q_f = q * jnp.exp(cum)
    q_dec = q_f.astype(BF)
    k_inv = (k * jnp.exp(-cum)).astype(BF)
    k_f = k * jnp.exp(tot_rows - cum)

    for h in range(heads):
        kl = slice(h * dk, (h + 1) * dk)
        vl = slice(h * dv, (h + 1) * dv)
        s = jnp.where(causal, _dot_nt(q_dec[:, kl], k_inv[:, kl]), 0.0).astype(BF)
        ov_ref[:, vl] = _dot(s, v[:, vl])

    head_of_lane = lax.broadcasted_iota(jnp.int32, (CHUNK, key), 1) // dk
    st = st_ref[...]
    for c in range(n_chunks):
        rows = slice(c * CHUNK, (c + 1) * CHUNK)
        qc = q_f[rows]
        kc = k_f[rows]
        qs = jnp.concatenate([jnp.where(head_of_lane == h, qc, 0.0) for h in range(heads)], axis=0).astype(BF)
        ks = jnp.concatenate([jnp.where(head_of_lane == h, kc, 0.0) for h in range(heads)], axis=0).astype(BF)
        vs = jnp.concatenate([v[rows, h * dv:(h + 1) * dv] for h in range(heads)], axis=0)
        o_int = _dot_nt(qs, st.astype(BF))
        for h in range(heads):
            ov_ref[rows, h * dv:(h + 1) * dv] += o_int[h * CHUNK:(h + 1) * CHUNK, :]
        st = st * jnp.exp(tots[c]) + _dot_tn(vs, ks)
    st_ref[...] = st

    r = _dot(xb, w_in_ref[:, 2 * key + val:2 * key + 2 * val])
    ng = ng_ref[...]
    normed = []
    for h in range(heads):
        oh = ov_ref[:, h * dv:(h + 1) * dv]
        ms = jnp.mean(oh * oh, axis=-1, keepdims=True)
        normed.append(oh * lax.rsqrt(ms + NORM_EPS) * ng)
    y = (jnp.concatenate(normed, axis=1) * _silu(r)).astype(BF)
    m = _dot(y, w_out_ref[...])
    o_ref[...] = _layer_norm(DN_ALPHA * x + m, lg_ref[...], lb_ref[...])


def _gla_layer(h, w_in, w_gate, b_gate, norm_g, w_out, ln_g, ln_b):
    b, l, d = h.shape
    rank, key = w_gate.shape
    val = w_out.shape[0]
    heads = GLA_HEADS
    tile = min(GLA_TILE, l)
    w_in_p = jnp.pad(w_in, ((0, 0), (0, LANES - rank))).astype(BF)
    w_gate_p = jnp.pad(w_gate, ((0, LANES - rank), (0, 0))).astype(BF)
    n_in = w_in_p.shape[1]
    body = functools.partial(_gla_body, key=key, val=val, heads=heads, sub=min(MIX_SUB, tile))
    return pl.pallas_call(
        body,
        grid=(b, l // tile),
        in_specs=[
            pl.BlockSpec((None, tile, d), lambda i, j: (i, j, 0)),
            _resident((d, n_in)),
            _resident((LANES, key)),
            _resident((1, key)),
            _resident((1, val // heads)),
            _resident((val, d)),
            _resident((1, d)),
            _resident((1, d)),
        ],
        out_specs=pl.BlockSpec((None, tile, d), lambda i, j: (i, j, 0)),
        out_shape=jax.ShapeDtypeStruct((b, l, d), F32),
        scratch_shapes=[
            pltpu.VMEM((val // heads, key), F32),
            pltpu.VMEM((tile, val), F32),
        ],
        compiler_params=pltpu.CompilerParams(
            dimension_semantics=("parallel", "arbitrary"), vmem_limit_bytes=VMEM_LIMIT),
        name="gla_ln",
    )(h, w_in_p, w_gate_p, b_gate.reshape(1, key), norm_g.reshape(1, -1), w_out.astype(BF),
      ln_g.reshape(1, d), ln_b.reshape(1, d))


def _ret_body(x_ref, cos_ref, sin_ref, w_in_ref, ng_ref, w_out_ref, lg_ref, lb_ref,
              o_ref, st_ref, ov_ref, *, key, val, heads, sub):
    @pl.when(pl.program_id(1) == 0)
    def _():
        st_ref[...] = jnp.zeros_like(st_ref)

    for s in range(x_ref.shape[0] // sub):
        rows = slice(s * sub, (s + 1) * sub)
        _ret_rows(x_ref.at[rows, :], cos_ref.at[rows, :], sin_ref.at[rows, :], w_in_ref, ng_ref, w_out_ref,
                  lg_ref, lb_ref, o_ref.at[rows, :], st_ref, ov_ref.at[rows, :], key=key, val=val, heads=heads)


def _ret_rows(x_ref, cos_ref, sin_ref, w_in_ref, ng_ref, w_out_ref, lg_ref, lb_ref,
              o_ref, st_ref, ov_ref, *, key, val, heads):
    dk = key // heads
    dv = val // heads
    half = dk // 2
    tile = x_ref.shape[0]

    x = x_ref[...]
    xb = x.astype(BF)
    cos = cos_ref[...]
    sin = sin_ref[...]
    ri = lax.broadcasted_iota(jnp.int32, (tile, tile), 0)
    ci = lax.broadcasted_iota(jnp.int32, (tile, tile), 1)
    causal = ci <= ri
    dist = (ri - ci).astype(F32)
    pos = lax.broadcasted_iota(jnp.int32, (tile, 1), 0).astype(F32)
    ng = ng_ref[...]

    def rope(t):
        t1 = t[:, :half]
        t2 = t[:, half:]
        return jnp.concatenate([t1 * cos - t2 * sin, t1 * sin + t2 * cos], axis=1)

    for h in range(heads):
        log_g = math.log(1.0 - 2.0 ** (-5.0 - h))
        qh = rope(_dot(xb, w_in_ref[:, h * dk:(h + 1) * dk]))
        kh = rope(_dot(xb, w_in_ref[:, key + h * dk:key + (h + 1) * dk])) * (dk ** -0.5)
        vh = _dot(xb, w_in_ref[:, 2 * key + h * dv:2 * key + (h + 1) * dv]).astype(BF)
        qb = qh.astype(BF)
        decay = jnp.where(causal, jnp.exp(dist * log_g), 0.0)
        s = (_dot_nt(qb, kh.astype(BF)) * decay).astype(BF)
        st = st_ref[h]
        o = _dot(s, vh) + _dot(qb, st.astype(BF)) * jnp.exp((pos + 1.0) * log_g)
        k_dec = (kh * jnp.exp((tile - 1.0 - pos) * log_g)).astype(BF)
        st_ref[h] = math.exp(tile * log_g) * st + _dot_tn(k_dec, vh)
        mu = jnp.mean(o, axis=-1, keepdims=True)
        oc = o - mu
        var = jnp.mean(oc * oc, axis=-1, keepdims=True)
        ov_ref[:, h * dv:(h + 1) * dv] = oc * lax.rsqrt(var + NORM_EPS) * ng

    g = _dot(xb, w_in_ref[:, 2 * key + val:2 * key + 2 * val])
    y = (ov_ref[...] * _silu(g)).astype(BF)
    m = _dot(y, w_out_ref[...])
    o_ref[...] = _layer_norm(DN_ALPHA * x + m, lg_ref[...], lb_ref[...])


def _ret_layer(h, w_in, norm_g, w_out, ln_g, ln_b):
    b, l, d = h.shape
    val = w_out.shape[0]
    key = (w_in.shape[1] - 2 * val) // 2
    heads = RET_HEADS
    dk = key // heads
    tile = min(RET_TILE, l)
    inv = ROPE_BASE ** (-jnp.arange(0, dk, 2, dtype=F32) / dk)
    ang = jnp.arange(l, dtype=F32)[:, None] * inv[None, :]
    cos = jnp.cos(ang)
    sin = jnp.sin(ang)
    body = functools.partial(_ret_body, key=key, val=val, heads=heads, sub=min(MIX_SUB, tile))
    return pl.pallas_call(
        body,
        grid=(b, l // tile),
        in_specs=[
            pl.BlockSpec((None, tile, d), lambda i, j: (i, j, 0)),
            pl.BlockSpec((tile, dk // 2), lambda i, j: (j, 0)),
            pl.BlockSpec((tile, dk // 2), lambda i, j: (j, 0)),
            _resident((d, w_in.shape[1])),
            _resident((1, val // heads)),
            _resident((val, d)),
            _resident((1, d)),
            _resident((1, d)),
        ],
        out_specs=pl.BlockSpec((None, tile, d), lambda i, j: (i, j, 0)),
        out_shape=jax.ShapeDtypeStruct((b, l, d), F32),
        scratch_shapes=[
            pltpu.VMEM((heads, dk, val // heads), F32),
            pltpu.VMEM((tile, val), F32),
        ],
        compiler_params=pltpu.CompilerParams(
            dimension_semantics=("parallel", "arbitrary"), vmem_limit_bytes=VMEM_LIMIT),
        name="ret_ln",
    )(h, cos, sin, w_in.astype(BF), norm_g.reshape(1, -1), w_out.astype(BF),
      ln_g.reshape(1, d), ln_b.reshape(1, d))


def _ssd_body(x_ref, w_in_ref, cw_ref, cb_ref, dtb_ref, alog_ref, dsk_ref, ng_ref, ex_ref,
              w_out_ref, lg_ref, lb_ref, o_ref, st_ref, cbuf_ref, y_ref, *, inner, groups, state, sub):
    tile = x_ref.shape[0]
    conv_dim = inner + 2 * groups * state

    @pl.when(pl.program_id(1) == 0)
    def _():
        st_ref[...] = jnp.zeros_like(st_ref)
        cbuf_ref[0:SUBLANES, :] = jnp.zeros((SUBLANES, conv_dim), F32)

    for s in range(tile // sub):
        rows = slice(s * sub, (s + 1) * sub)
        _ssd_rows(x_ref.at[rows, :], w_in_ref, cw_ref, cb_ref, dtb_ref, alog_ref, dsk_ref, ng_ref, ex_ref,
                  w_out_ref, lg_ref, lb_ref, o_ref.at[rows, :], st_ref,
                  cbuf_ref.at[s * sub:(s + 1) * sub + SUBLANES, :], y_ref.at[rows, :],
                  inner=inner, groups=groups, state=state)
    cbuf_ref[0:SUBLANES, :] = cbuf_ref[tile:tile + SUBLANES, :]


def _ssd_rows(x_ref, w_in_ref, cw_ref, cb_ref, dtb_ref, alog_ref, dsk_ref, ng_ref, ex_ref,
              w_out_ref, lg_ref, lb_ref, o_ref, st_ref, cbuf_ref, y_ref, *, inner, groups, state):
    tile = x_ref.shape[0]
    gw = inner // groups
    pair = 2 * SSD_HEADDIM
    conv_dim = inner + 2 * groups * state

    x = x_ref[...]
    xb = x.astype(BF)

    dt_raw = _dot(xb, w_in_ref[:, inner + conv_dim:inner + conv_dim + LANES])
    cbuf_ref[SUBLANES:SUBLANES + tile, :] = _dot(xb, w_in_ref[:, inner:inner + conv_dim])
    dt_h = _softplus(dt_raw + dtb_ref[...])
    da_h = dt_h * (-jnp.exp(alog_ref[...]))
    cum_h = _sel_dot_lhs(_chunk_tril(tile, CHUNK), da_h, 3)
    z = _dot(xb, w_in_ref[:, 0:inner])
    n_heads = inner // SSD_HEADDIM
    dt = _dot(_lane_group_pieces(dt_h, n_heads), ex_ref[...])
    cum = _dot(_lane_group_pieces(cum_h, n_heads), ex_ref[...])

    win = cbuf_ref[...]
    win1 = pltpu.roll(win, 1, 0)
    pair_b = cw_ref[1:2, :] * win + cw_ref[0:1, :] * win1
    pair_a = cw_ref[3:4, :] * win[SUBLANES:] + cw_ref[2:3, :] * win1[SUBLANES:]
    acc = pair_a + pltpu.roll(pair_b, 2, 0)[SUBLANES:] + cb_ref[...]
    xbc = _silu(acc)
    xs = xbc[:, :inner]
    bm = xbc[:, inner:inner + groups * state].astype(BF)
    cm = xbc[:, inner + groups * state:].astype(BF)
    xdt = xs * dt

    ri = lax.broadcasted_iota(jnp.int32, (CHUNK, pair), 0)
    ci = lax.broadcasted_iota(jnp.int32, (CHUNK, pair), 1)
    cj = ci % CHUNK
    causal = cj <= ri
    diag = cj == ri
    lane = lax.broadcasted_iota(jnp.int32, (CHUNK, pair), 1)
    first = lane < SSD_HEADDIM
    dsk = dsk_ref[...]

    for c in range(tile // CHUNK):
        rows = slice(c * CHUNK, (c + 1) * CHUNK)
        cum_c = cum[rows]
        last = cum_c[CHUNK - 1:CHUNK, :]
        e_cum = jnp.exp(cum_c)
        x_end = (xdt[rows] * jnp.exp(last - cum_c)).astype(BF)
        dec = jnp.exp(last)
        for g in range(groups):
            sl = slice(g * state, (g + 1) * state)
            gl = slice(g * gw, (g + 1) * gw)
            cg = cm[rows, sl]
            bg = bm[rows, sl]
            cb2 = _dot_nt(cg, jnp.concatenate([bg, bg], axis=0))
            st = st_ref[g]
            y_state = _dot(cg, st.astype(BF)) * e_cum[:, gl]
            st_ref[g] = st * dec[:, gl] + _dot_tn(bg, x_end[:, gl])
            ys = []
            for p in range(gw // pair):
                pl_ = slice(g * gw + p * pair, g * gw + (p + 1) * pair)
                col = cum_c[:, pl_]
                row = jnp.sum(jnp.where(diag, col, 0.0), axis=0, keepdims=True)
                seg = jnp.exp(jnp.where(causal, col - row, -1e30))
                w = (seg * cb2).astype(BF)
                xp = xdt[rows, pl_].astype(BF)
                zero = jnp.zeros_like(xp)
                rhs = jnp.concatenate([jnp.where(first, xp, zero), jnp.where(first, zero, xp)], axis=0)
                ys.append(_dot(w, rhs))
            y_ref[rows, gl] = jnp.concatenate(ys, axis=1) + y_state + xs[rows, gl] * dsk[:, gl]

    ng = ng_ref[...]
    normed = []
    for g in range(groups):
        gl = slice(g * gw, (g + 1) * gw)
        yg = y_ref[:, gl] * _silu(z[:, gl])
        ms = jnp.mean(yg * yg, axis=-1, keepdims=True)
        normed.append((yg * lax.rsqrt(ms + NORM_EPS) * ng[:, gl]).astype(BF))
    m = _dot(jnp.concatenate(normed, axis=1), w_out_ref[...])
    o_ref[...] = _layer_norm(DN_ALPHA * x + m, lg_ref[...], lb_ref[...])


def _ssd_layer(h, w_in, conv_w, conv_b, dt_bias, a_log, d_skip, norm_g, w_out, ln_g, ln_b):
    b, l, d = h.shape
    inner = w_out.shape[0]
    heads = dt_bias.shape[0]
    groups = SSD_GROUPS
    state = SSD_STATE
    conv_dim = conv_w.shape[1]
    assert inner // heads == SSD_HEADDIM and conv_dim == inner + 2 * groups * state
    tile = min(SSD_TILE, l)
    reps = LANES // heads
    assert reps * heads == LANES and reps >= 3
    w_in_p = jnp.concatenate([w_in[:, :inner + conv_dim]] + [w_in[:, inner + conv_dim:]] * reps, axis=1).astype(BF)
    lane_pad = lambda t: jnp.tile(t, reps).reshape(1, LANES)
    lane_head = jnp.where(jnp.arange(LANES) < 3 * heads, jnp.arange(LANES) % heads, -1)
    expand = (lane_head[:, None] == (jnp.arange(inner) // SSD_HEADDIM)[None, :]).astype(BF)
    rep = lambda t: jnp.repeat(t, SSD_HEADDIM).reshape(1, inner)
    body = functools.partial(_ssd_body, inner=inner, groups=groups, state=state, sub=min(MIX_SUB, tile))
    return pl.pallas_call(
        body,
        grid=(b, l // tile),
        in_specs=[
            pl.BlockSpec((None, tile, d), lambda i, j: (i, j, 0)),
            _resident((d, w_in_p.shape[1])),
            _resident((SSD_CONV, conv_dim)),
            _resident((1, conv_dim)),
            _resident((1, LANES)),
            _resident((1, LANES)),
            _resident((1, inner)),
            _resident((1, inner)),
            _resident((LANES, inner)),
            _resident((inner, d)),
            _resident((1, d)),
            _resident((1, d)),
        ],
        out_specs=pl.BlockSpec((None, tile, d), lambda i, j: (i, j, 0)),
        out_shape=jax.ShapeDtypeStruct((b, l, d), F32),
        scratch_shapes=[
            pltpu.VMEM((groups, state, inner // groups), F32),
            pltpu.VMEM((SUBLANES + tile, conv_dim), F32),
            pltpu.VMEM((tile, inner), F32),
        ],
        compiler_params=pltpu.CompilerParams(
            dimension_semantics=("parallel", "arbitrary"), vmem_limit_bytes=VMEM_LIMIT),
        name="ssd_ln",
    )(h, w_in_p, conv_w, conv_b.reshape(1, conv_dim), lane_pad(dt_bias), lane_pad(a_log), rep(d_skip),
      norm_g.reshape(1, inner), expand, w_out.astype(BF), ln_g.reshape(1, d), ln_b.reshape(1, d))


def kernel(x, ffn_w_in, ffn_w_out, ln_g, ln_b, gla_w_in, gla_w_gate, gla_b_gate, gla_norm_g, gla_w_out,
           ret_w_in, ret_norm_g, ret_w_out, ssd_w_in, ssd_conv_w, ssd_conv_b, ssd_dt_bias, ssd_a_log,
           ssd_d, ssd_norm_g, ssd_w_out):
    b, l, d = x.shape
    h = x
    for i in range(DEPTH):
        h = _ffn_layer(h.reshape(b * l, d), ffn_w_in[i, 0], ffn_w_out[i, 0], ln_g[i, 0], ln_b[i, 0])
        h = h.reshape(b, l, d)
        kind, j = i % N_MIXERS, i // N_MIXERS
        if kind == 0:
            h = _gla_layer(h, gla_w_in[j], gla_w_gate[j], gla_b_gate[j], gla_norm_g[j], gla_w_out[j],
                           ln_g[i, 1], ln_b[i, 1])
        elif kind == 1:
            h = _ret_layer(h, ret_w_in[j], ret_norm_g[j], ret_w_out[j], ln_g[i, 1], ln_b[i, 1])
        else:
            h = _ssd_layer(h, ssd_w_in[j], ssd_conv_w[j], ssd_conv_b[j], ssd_dt_bias[j], ssd_a_log[j],
                           ssd_d[j], ssd_norm_g[j], ssd_w_out[j], ln_g[i, 1], ln_b[i, 1])
        h = _ffn_layer(h.reshape(b * l, d), ffn_w_in[i, 1], ffn_w_out[i, 1], ln_g[i, 2], ln_b[i, 2])
        h = h.reshape(b, l, d)
    return h
```

```python
import functools
import math

import jax
import jax.numpy as jnp
from jax import lax
from jax.experimental import pallas as pl
from jax.experimental.pallas import tpu as pltpu

BF = jnp.bfloat16
F32 = jnp.float32

LANES = 128
SUBLANES = 8
VMEM_LIMIT = 56 * 1024 * 1024

DEPTH = 4
N_MIXERS = 3
CHUNK = 64
LN_EPS = 1e-5
NORM_EPS = 1e-6
DN_ALPHA = (2 * DEPTH) ** 0.25

GLA_HEADS = 4
GLA_TAU = 16.0
RET_HEADS = 4
ROPE_BASE = 10000.0
SSD_HEADDIM = 64
SSD_GROUPS = 4
SSD_STATE = 128
SSD_CONV = 4

FFN_TILE = 1024
FFN_SUB = 256
GLA_TILE = 512
RET_TILE = 512
SSD_TILE = 256
MIX_SUB = 256


def _dot(a, b):
    return jnp.dot(a, b, preferred_element_type=F32)


def _dot_nt(a, b):
    return lax.dot_general(a, b, (((1,), (1,)), ((), ())), preferred_element_type=F32)


def _dot_tn(a, b):
    return lax.dot_general(a, b, (((0,), (0,)), ((), ())), preferred_element_type=F32)


def _split_bf16(x, parts):
    out = []
    r = x
    for i in range(parts):
        p = r.astype(BF)
        out.append(p)
        if i + 1 < parts:
            r = r - p.astype(F32)
    return out


def _sel_dot_lhs(sel, x, parts):
    acc = None
    for p in _split_bf16(x, parts):
        t = _dot(sel, p)
        acc = t if acc is None else acc + t
    return acc


def _sel_dot_rhs(x, sel, parts):
    acc = None
    for p in _split_bf16(x, parts):
        t = _dot(p, sel)
        acc = t if acc is None else acc + t
    return acc


def _lane_group_pieces(x, group):
    grp = lax.broadcasted_iota(jnp.int32, x.shape, 1) // group
    p0 = x.astype(BF).astype(F32)
    r1 = x - p0
    p1 = r1.astype(BF).astype(F32)
    r2 = r1 - p1
    return jnp.where(grp == 0, p0, jnp.where(grp == 1, p1, jnp.where(grp == 2, r2, 0.0))).astype(BF)


def _sigmoid(x):
    return 1.0 / (1.0 + jnp.exp(-x))


def _silu(x):
    return x * _sigmoid(x)


def _softplus(x):
    return jnp.maximum(x, 0.0) + jnp.log(1.0 + jnp.exp(-jnp.abs(x)))


def _layer_norm(z, g, b):
    mu = jnp.mean(z, axis=-1, keepdims=True)
    zc = z - mu
    var = jnp.mean(zc * zc, axis=-1, keepdims=True)
    return zc * lax.rsqrt(var + LN_EPS) * g + b


def _chunk_causal(tile, chunk):
    ri = lax.broadcasted_iota(jnp.int32, (tile, tile), 0)
    ci = lax.broadcasted_iota(jnp.int32, (tile, tile), 1)
    return ((ri // chunk) == (ci // chunk)) & (ci <= ri)


def _chunk_tril(tile, chunk):
    return jnp.where(_chunk_causal(tile, chunk), 1.0, 0.0).astype(BF)


def _resident(shape):
    nd = len(shape)
    return pl.BlockSpec(shape, lambda *_: (0,) * nd, pipeline_mode=pl.Buffered(1))


def _ffn_body(x_ref, w_in_ref, w_out_ref, g_ref, b_ref, o_ref, *, d_ff, sub):
    for s in range(x_ref.shape[0] // sub):
        rows = slice(s * sub, (s + 1) * sub)
        x = x_ref[rows, :]
        gu = _dot(x.astype(BF), w_in_ref[...])
        act = (_silu(gu[:, :d_ff]) * gu[:, d_ff:]).astype(BF)
        y = _dot(act, w_out_ref[...])
        o_ref[rows, :] = _layer_norm(DN_ALPHA * x + 0.5 * y, g_ref[...], b_ref[...])


def _ffn_layer(h, w_in_all, w_out_all, layer, half, g, b):
    t, d = h.shape
    d_ff = w_out_all.shape[2]
    tm = min(FFN_TILE, t)
    pick = lambda *_: (layer, half, 0, 0)
    return pl.pallas_call(
        functools.partial(_ffn_body, d_ff=d_ff, sub=min(FFN_SUB, tm)),
        grid=(t // tm,),
        in_specs=[
            pl.BlockSpec((tm, d), lambda i: (i, 0)),
            pl.BlockSpec((None, None, d, 2 * d_ff), pick, pipeline_mode=pl.Buffered(1)),
            pl.BlockSpec((None, None, d_ff, d), pick, pipeline_mode=pl.Buffered(1)),
            _resident((1, d)),
            _resident((1, d)),
        ],
        out_specs=pl.BlockSpec((tm, d), lambda i: (i, 0)),
        out_shape=jax.ShapeDtypeStruct((t, d), F32),
        compiler_params=pltpu.CompilerParams(
            dimension_semantics=("parallel",), vmem_limit_bytes=VMEM_LIMIT),
        name="ffn_ln",
    )(h, w_in_all, w_out_all, g.reshape(1, d), b.reshape(1, d))


def _gla_body(x_ref, w_in_ref, w_gate_ref, b_gate_ref, ng_ref, w_out_ref, lg_ref, lb_ref,
              o_ref, st_ref, ov_ref, *, key, val, heads, sub):
    @pl.when(pl.program_id(1) == 0)
    def _():
        st_ref[...] = jnp.zeros_like(st_ref)

    def finish(rows, x, y):
        m = _dot(y, w_out_ref[...])
        o_ref[rows, :] = _layer_norm(DN_ALPHA * x + m, lg_ref[...], lb_ref[...])

    pending = None
    for s in range(x_ref.shape[0] // sub):
        rows = slice(s * sub, (s + 1) * sub)
        cur = (rows,) + _gla_rows(x_ref.at[rows, :], w_in_ref, w_gate_ref, b_gate_ref, ng_ref, st_ref,
                                  ov_ref.at[rows, :], key=key, val=val, heads=heads)
        if pending is not None:
            finish(*pending)
        pending = cur
    finish(*pending)


def _gla_rows(x_ref, w_in_ref, w_gate_ref, b_gate_ref, ng_ref, st_ref, ov_ref, *, key, val, heads):
    dk = key // heads
    dv = val // heads
    tile = x_ref.shape[0]

    x = x_ref[...]
    xb = x.astype(BF)
    g_low = _dot(xb, w_in_ref[:, 2 * key + 2 * val:2 * key + 2 * val + LANES]).astype(BF)
    q = _dot(xb, w_in_ref[:, 0:key]) * (dk ** -0.5)
    zg = _dot(g_low, w_gate_ref[...]) + b_gate_ref[...]
    k = _dot(xb, w_in_ref[:, key:2 * key])
    log_a = (jnp.minimum(zg, 0.0) - jnp.log(1.0 + jnp.exp(-jnp.abs(zg)))) * (1.0 / GLA_TAU)
    causal = _chunk_causal(tile, CHUNK)
    cum = _sel_dot_lhs(jnp.where(causal, 1.0, 0.0).astype(BF), log_a, 3)
    v = _dot(xb, w_in_ref[:, 2 * key:2 * key + val]).astype(BF)

    n_chunks = tile // CHUNK
    tots = [cum[(c + 1) * CHUNK - 1:(c + 1) * CHUNK, :] for c in range(n_chunks)]
    tot_rows = jnp.concatenate([jnp.broadcast_to(t, (CHUNK, key)) for t in tots], axis=0)
    q_f = q * jnp.exp(cum)
    q_dec = q_f.astype(BF)
    k_inv = (k * jnp.exp(-cum)).astype(BF)
    k_f = k * jnp.exp(tot_rows - cum)

    for h in range(heads):
        kl = slice(h * dk, (h + 1) * dk)
        vl = slice(h * dv, (h + 1) * dv)
        s = jnp.where(causal, _dot_nt(q_dec[:, kl], k_inv[:, kl]), 0.0).astype(BF)
        ov_ref[:, vl] = _dot(s, v[:, vl])

    head_of_lane = lax.broadcasted_iota(jnp.int32, (CHUNK, key), 1) // dk
    st = st_ref[...]
    for c in range(n_chunks):
        rows = slice(c * CHUNK, (c + 1) * CHUNK)
        qc = q_f[rows]
        kc = k_f[rows]
        qs = jnp.concatenate([jnp.where(head_of_lane == h, qc, 0.0) for h in range(heads)], axis=0).astype(BF)
        ks = jnp.concatenate([jnp.where(head_of_lane == h, kc, 0.0) for h in range(heads)], axis=0).astype(BF)
        vs = jnp.concatenate([v[rows, h * dv:(h + 1) * dv] for h in range(heads)], axis=0)
        o_int = _dot_nt(qs, st.astype(BF))
        for h in range(heads):
            ov_ref[rows, h * dv:(h + 1) * dv] += o_int[h * CHUNK:(h + 1) * CHUNK, :]
        st = st * jnp.exp(tots[c]) + _dot_tn(vs, ks)
    st_ref[...] = st

    r = _dot(xb, w_in_ref[:, 2 * key + val:2 * key + 2 * val])
    ng = ng_ref[...]
    normed = []
    for h in range(heads):
        oh = ov_ref[:, h * dv:(h + 1) * dv]
        ms = jnp.mean(oh * oh, axis=-1, keepdims=True)
        normed.append(oh * lax.rsqrt(ms + NORM_EPS) * ng)
    y = (jnp.concatenate(normed, axis=1) * _silu(r)).astype(BF)
    return x, y


def _gla_layer(h, w_in, w_gate, b_gate, norm_g, w_out, ln_g, ln_b):
    b, l, d = h.shape
    rank, key = w_gate.shape
    val = w_out.shape[0]
    heads = GLA_HEADS
    tile = min(GLA_TILE, l)
    w_in_p = jnp.pad(w_in, ((0, 0), (0, LANES - rank))).astype(BF)
    w_gate_p = jnp.pad(w_gate, ((0, LANES - rank), (0, 0))).astype(BF)
    n_in = w_in_p.shape[1]
    body = functools.partial(_gla_body, key=key, val=val, heads=heads, sub=min(MIX_SUB, tile))
    return pl.pallas_call(
        body,
        grid=(b, l // tile),
        in_specs=[
            pl.BlockSpec((None, tile, d), lambda i, j: (i, j, 0)),
            _resident((d, n_in)),
            _resident((LANES, key)),
            _resident((1, key)),
            _resident((1, val // heads)),
            _resident((val, d)),
            _resident((1, d)),
            _resident((1, d)),
        ],
        out_specs=pl.BlockSpec((None, tile, d), lambda i, j: (i, j, 0)),
        out_shape=jax.ShapeDtypeStruct((b, l, d), F32),
        scratch_shapes=[
            pltpu.VMEM((val // heads, key), F32),
            pltpu.VMEM((tile, val), F32),
        ],
        compiler_params=pltpu.CompilerParams(
            dimension_semantics=("parallel", "arbitrary"), vmem_limit_bytes=VMEM_LIMIT),
        name="gla_ln",
    )(h, w_in_p, w_gate_p, b_gate.reshape(1, key), norm_g.reshape(1, -1), w_out.astype(BF),
      ln_g.reshape(1, d), ln_b.reshape(1, d))


def _ret_body(x_ref, cos_ref, sin_ref, w_in_ref, ng_ref, w_out_ref, lg_ref, lb_ref,
              o_ref, st_ref, ov_ref, *, key, val, heads, sub):
    @pl.when(pl.program_id(1) == 0)
    def _():
        st_ref[...] = jnp.zeros_like(st_ref)

    for s in range(x_ref.shape[0] // sub):
        rows = slice(s * sub, (s + 1) * sub)
        _ret_rows(x_ref.at[rows, :], cos_ref.at[rows, :], sin_ref.at[rows, :], w_in_ref, ng_ref, w_out_ref,
                  lg_ref, lb_ref, o_ref.at[rows, :], st_ref, ov_ref.at[rows, :], key=key, val=val, heads=heads)


def _ret_rows(x_ref, cos_ref, sin_ref, w_in_ref, ng_ref, w_out_ref, lg_ref, lb_ref,
              o_ref, st_ref, ov_ref, *, key, val, heads):
    dk = key // heads
    dv = val // heads
    half = dk // 2
    tile = x_ref.shape[0]

    x = x_ref[...]
    xb = x.astype(BF)
    cos = cos_ref[...]
    sin = sin_ref[...]
    ri = lax.broadcasted_iota(jnp.int32, (tile, tile), 0)
    ci = lax.broadcasted_iota(jnp.int32, (tile, tile), 1)
    causal = ci <= ri
    dist = (ri - ci).astype(F32)
    pos = lax.broadcasted_iota(jnp.int32, (tile, 1), 0).astype(F32)
    ng = ng_ref[...]

    def rope(t):
        t1 = t[:, :half]
        t2 = t[:, half:]
        return jnp.concatenate([t1 * cos - t2 * sin, t1 * sin + t2 * cos], axis=1)

    for h in range(heads):
        log_g = math.log(1.0 - 2.0 ** (-5.0 - h))
        qh = rope(_dot(xb, w_in_ref[:, h * dk:(h + 1) * dk]))
        kh = rope(_dot(xb, w_in_ref[:, key + h * dk:key + (h + 1) * dk])) * (dk ** -0.5)
        vh = _dot(xb, w_in_ref[:, 2 * key + h * dv:2 * key + (h + 1) * dv]).astype(BF)
        qb = qh.astype(BF)
        decay = jnp.where(causal, jnp.exp(dist * log_g), 0.0)
        s = (_dot_nt(qb, kh.astype(BF)) * decay).astype(BF)
        st = st_ref[h]
        o = _dot(s, vh) + _dot(qb, st.astype(BF)) * jnp.exp((pos + 1.0) * log_g)
        k_dec = (kh * jnp.exp((tile - 1.0 - pos) * log_g)).astype(BF)
        st_ref[h] = math.exp(tile * log_g) * st + _dot_tn(k_dec, vh)
        mu = jnp.mean(o, axis=-1, keepdims=True)
        oc = o - mu
        var = jnp.mean(oc * oc, axis=-1, keepdims=True)
        ov_ref[:, h * dv:(h + 1) * dv] = oc * lax.rsqrt(var + NORM_EPS) * ng

    g = _dot(xb, w_in_ref[:, 2 * key + val:2 * key + 2 * val])
    y = (ov_ref[...] * _silu(g)).astype(BF)
    m = _dot(y, w_out_ref[...])
    o_ref[...] = _layer_norm(DN_ALPHA * x + m, lg_ref[...], lb_ref[...])


def _ret_layer(h, w_in, norm_g, w_out, ln_g, ln_b):
    b, l, d = h.shape
    val = w_out.shape[0]
    key = (w_in.shape[1] - 2 * val) // 2
    heads = RET_HEADS
    dk = key // heads
    tile = min(RET_TILE, l)
    inv = ROPE_BASE ** (-jnp.arange(0, dk, 2, dtype=F32) / dk)
    ang = jnp.arange(l, dtype=F32)[:, None] * inv[None, :]
    cos = jnp.cos(ang)
    sin = jnp.sin(ang)
    body = functools.partial(_ret_body, key=key, val=val, heads=heads, sub=min(MIX_SUB, tile))
    return pl.pallas_call(
        body,
        grid=(b, l // tile),
        in_specs=[
            pl.BlockSpec((None, tile, d), lambda i, j: (i, j, 0)),
            pl.BlockSpec((tile, dk // 2), lambda i, j: (j, 0)),
            pl.BlockSpec((tile, dk // 2), lambda i, j: (j, 0)),
            _resident((d, w_in.shape[1])),
            _resident((1, val // heads)),
            _resident((val, d)),
            _resident((1, d)),
            _resident((1, d)),
        ],
        out_specs=pl.BlockSpec((None, tile, d), lambda i, j: (i, j, 0)),
        out_shape=jax.ShapeDtypeStruct((b, l, d), F32),
        scratch_shapes=[
            pltpu.VMEM((heads, dk, val // heads), F32),
            pltpu.VMEM((tile, val), F32),
        ],
        compiler_params=pltpu.CompilerParams(
            dimension_semantics=("parallel", "arbitrary"), vmem_limit_bytes=VMEM_LIMIT),
        name="ret_ln",
    )(h, cos, sin, w_in.astype(BF), norm_g.reshape(1, -1), w_out.astype(BF),
      ln_g.reshape(1, d), ln_b.reshape(1, d))


def _ssd_body(x_ref, w_in_ref, cw_ref, cb_ref, dtb_ref, alog_ref, dsk_ref, ng_ref, ex_ref,
              w_out_ref, lg_ref, lb_ref, o_ref, st_ref, cbuf_ref, y_ref, *, inner, groups, state, sub):
    tile = x_ref.shape[0]
    conv_dim = inner + 2 * groups * state

    @pl.when(pl.program_id(1) == 0)
    def _():
        st_ref[...] = jnp.zeros_like(st_ref)
        cbuf_ref[0:SUBLANES, :] = jnp.zeros((SUBLANES, conv_dim), F32)

    def finish(rows, x, y):
        m = _dot(y, w_out_ref[...])
        o_ref[rows, :] = _layer_norm(DN_ALPHA * x + m, lg_ref[...], lb_ref[...])

    pending = None
    for s in range(tile // sub):
        rows = slice(s * sub, (s + 1) * sub)
        cur = (rows,) + _ssd_rows(x_ref.at[rows, :], w_in_ref, cw_ref, cb_ref, dtb_ref, alog_ref, dsk_ref,
                                  ng_ref, ex_ref, st_ref, cbuf_ref.at[s * sub:(s + 1) * sub + SUBLANES, :],
                                  y_ref.at[rows, :], inner=inner, groups=groups, state=state)
        if pending is not None:
            finish(*pending)
        pending = cur
    finish(*pending)
    cbuf_ref[0:SUBLANES, :] = cbuf_ref[tile:tile + SUBLANES, :]


def _ssd_rows(x_ref, w_in_ref, cw_ref, cb_ref, dtb_ref, alog_ref, dsk_ref, ng_ref, ex_ref,
              st_ref, cbuf_ref, y_ref, *, inner, groups, state):
    tile = x_ref.shape[0]
    gw = inner // groups
    pair = 2 * SSD_HEADDIM
    conv_dim = inner + 2 * groups * state

    x = x_ref[...]
    xb = x.astype(BF)

    dt_raw = _dot(xb, w_in_ref[:, inner + conv_dim:inner + conv_dim + LANES])
    cbuf_ref[SUBLANES:SUBLANES + tile, :] = _dot(xb, w_in_ref[:, inner:inner + conv_dim])
    dt_h = _softplus(dt_raw + dtb_ref[...])
    da_h = dt_h * (-jnp.exp(alog_ref[...]))
    cum_h = _sel_dot_lhs(_chunk_tril(tile, CHUNK), da_h, 3)
    z = _dot(xb, w_in_ref[:, 0:inner])
    n_heads = inner // SSD_HEADDIM
    dt = _dot(_lane_group_pieces(dt_h, n_heads), ex_ref[...])
    cum = _dot(_lane_group_pieces(cum_h, n_heads), ex_ref[...])

    win = cbuf_ref[...]
    win1 = pltpu.roll(win, 1, 0)
    pair_b = cw_ref[1:2, :] * win + cw_ref[0:1, :] * win1
    pair_a = cw_ref[3:4, :] * win[SUBLANES:] + cw_ref[2:3, :] * win1[SUBLANES:]
    acc = pair_a + pltpu.roll(pair_b, 2, 0)[SUBLANES:] + cb_ref[...]
    xbc = _silu(acc)
    xs = xbc[:, :inner]
    bm = xbc[:, inner:inner + groups * state].astype(BF)
    cm = xbc[:, inner + groups * state:].astype(BF)
    xdt = xs * dt

    ri = lax.broadcasted_iota(jnp.int32, (CHUNK, pair), 0)
    ci = lax.broadcasted_iota(jnp.int32, (CHUNK, pair), 1)
    cj = ci % CHUNK
    causal = cj <= ri
    diag = cj == ri
    lane = lax.broadcasted_iota(jnp.int32, (CHUNK, pair), 1)
    first = lane < SSD_HEADDIM
    dsk = dsk_ref[...]

    n_chunks = tile // CHUNK
    cb_all = []
    for g in range(groups):
        sl = slice(g * state, (g + 1) * state)
        b_twice = jnp.concatenate(
            [bm[c * CHUNK:(c + 1) * CHUNK, sl] for c in range(n_chunks) for _ in range(2)], axis=0)
        cb_all.append(_dot_nt(cm[:, sl], b_twice))

    for c in range(n_chunks):
        rows = slice(c * CHUNK, (c + 1) * CHUNK)
        cum_c = cum[rows]
        last = cum_c[CHUNK - 1:CHUNK, :]
        e_cum = jnp.exp(cum_c)
        x_end = (xdt[rows] * jnp.exp(last - cum_c)).astype(BF)
        dec = jnp.exp(last)
        for g in range(groups):
            sl = slice(g * state, (g + 1) * state)
            gl = slice(g * gw, (g + 1) * gw)
            cg = cm[rows, sl]
            bg = bm[rows, sl]
            cb2 = cb_all[g][rows, c * pair:(c + 1) * pair]
            st = st_ref[g]
            y_state = _dot(cg, st.astype(BF)) * e_cum[:, gl]
            st_ref[g] = st * dec[:, gl] + _dot_tn(bg, x_end[:, gl])
            ys = []
            for p in range(gw // pair):
                pl_ = slice(g * gw + p * pair, g * gw + (p + 1) * pair)
                col = cum_c[:, pl_]
                row = jnp.sum(jnp.where(diag, col, 0.0), axis=0, keepdims=True)
                seg = jnp.exp(jnp.where(causal, col - row, -1e30))
                w = (seg * cb2).astype(BF)
                xp = xdt[rows, pl_].astype(BF)
                zero = jnp.zeros_like(xp)
                rhs = jnp.concatenate([jnp.where(first, xp, zero), jnp.where(first, zero, xp)], axis=0)
                ys.append(_dot(w, rhs))
            y_ref[rows, gl] = jnp.concatenate(ys, axis=1) + y_state + xs[rows, gl] * dsk[:, gl]

    ng = ng_ref[...]
    normed = []
    for g in range(groups):
        gl = slice(g * gw, (g + 1) * gw)
        yg = y_ref[:, gl] * _silu(z[:, gl])
        ms = jnp.mean(yg * yg, axis=-1, keepdims=True)
        normed.append((yg * lax.rsqrt(ms + NORM_EPS) * ng[:, gl]).astype(BF))
    return x, jnp.concatenate(normed, axis=1)


def _ssd_layer(h, w_in, conv_w, conv_b, dt_bias, a_log, d_skip, norm_g, w_out, ln_g, ln_b):
    b, l, d = h.shape
    inner = w_out.shape[0]
    heads = dt_bias.shape[0]
    groups = SSD_GROUPS
    state = SSD_STATE
    conv_dim = conv_w.shape[1]
    assert inner // heads == SSD_HEADDIM and conv_dim == inner + 2 * groups * state
    tile = min(SSD_TILE, l)
    reps = LANES // heads
    assert reps * heads == LANES and reps >= 3
    w_in_p = jnp.concatenate([w_in[:, :inner + conv_dim]] + [w_in[:, inner + conv_dim:]] * reps, axis=1).astype(BF)
    lane_pad = lambda t: jnp.tile(t, reps).reshape(1, LANES)
    lane_head = jnp.where(jnp.arange(LANES) < 3 * heads, jnp.arange(LANES) % heads, -1)
    expand = (lane_head[:, None] == (jnp.arange(inner) // SSD_HEADDIM)[None, :]).astype(BF)
    rep = lambda t: jnp.repeat(t, SSD_HEADDIM).reshape(1, inner)
    body = functools.partial(_ssd_body, inner=inner, groups=groups, state=state, sub=min(MIX_SUB, tile))
    return pl.pallas_call(
        body,
        grid=(b, l // tile),
        in_specs=[
            pl.BlockSpec((None, tile, d), lambda i, j: (i, j, 0)),
            _resident((d, w_in_p.shape[1])),
            _resident((SSD_CONV, conv_dim)),
            _resident((1, conv_dim)),
            _resident((1, LANES)),
            _resident((1, LANES)),
            _resident((1, inner)),
            _resident((1, inner)),
            _resident((LANES, inner)),
            _resident((inner, d)),
            _resident((1, d)),
            _resident((1, d)),
        ],
        out_specs=pl.BlockSpec((None, tile, d), lambda i, j: (i, j, 0)),
        out_shape=jax.ShapeDtypeStruct((b, l, d), F32),
        scratch_shapes=[
            pltpu.VMEM((groups, state, inner // groups), F32),
            pltpu.VMEM((SUBLANES + tile, conv_dim), F32),
            pltpu.VMEM((tile, inner), F32),
        ],
        compiler_params=pltpu.CompilerParams(
            dimension_semantics=("parallel", "arbitrary"), vmem_limit_bytes=VMEM_LIMIT),
        name="ssd_ln",
    )(h, w_in_p, conv_w, conv_b.reshape(1, conv_dim), lane_pad(dt_bias), lane_pad(a_log), rep(d_skip),
      norm_g.reshape(1, inner), expand, w_out.astype(BF), ln_g.reshape(1, d), ln_b.reshape(1, d))


def kernel(x, ffn_w_in, ffn_w_out, ln_g, ln_b, gla_w_in, gla_w_gate, gla_b_gate, gla_norm_g, gla_w_out,
           ret_w_in, ret_norm_g, ret_w_out, ssd_w_in, ssd_conv_w, ssd_conv_b, ssd_dt_bias, ssd_a_log,
           ssd_d, ssd_norm_g, ssd_w_out):
    b, l, d = x.shape
    h = x
    ffn_w_in = ffn_w_in.astype(BF)
    ffn_w_out = ffn_w_out.astype(BF)
    for i in range(DEPTH):
        h = _ffn_layer(h.reshape(b * l, d), ffn_w_in, ffn_w_out, i, 0, ln_g[i, 0], ln_b[i, 0])
        h = h.reshape(b, l, d)
        kind, j = i % N_MIXERS, i // N_MIXERS
        if kind == 0:
            h = _gla_layer(h, gla_w_in[j], gla_w_gate[j], gla_b_gate[j], gla_norm_g[j], gla_w_out[j],
                           ln_g[i, 1], ln_b[i, 1])
        elif kind == 1:
            h = _ret_layer(h, ret_w_in[j], ret_norm_g[j], ret_w_out[j], ln_g[i, 1], ln_b[i, 1])
        else:
            h = _ssd_layer(h, ssd_w_in[j], ssd_conv_w[j], ssd_conv_b[j], ssd_dt_bias[j], ssd_a_log[j],
                           ssd_d[j], ssd_norm_g[j], ssd_w_out[j], ln_g[i, 1], ln_b[i, 1])
        h = _ffn_layer(h.reshape(b * l, d), ffn_w_in, ffn_w_out, i, 1, ln_g[i, 2], ln_b[i, 2])
        h = h.reshape(b, l, d)
    return h
```

```python
import functools
import math

import jax
import jax.numpy as jnp
from jax import lax
from jax.experimental import pallas as pl
from jax.experimental.pallas import tpu as pltpu

BF = jnp.bfloat16
F32 = jnp.float32

LANES = 128
SUBLANES = 8
VMEM_LIMIT = 56 * 1024 * 1024

DEPTH = 4
N_MIXERS = 3
CHUNK = 64
LN_EPS = 1e-5
NORM_EPS = 1e-6
DN_ALPHA = (2 * DEPTH) ** 0.25

GLA_HEADS = 4
GLA_TAU = 16.0
RET_HEADS = 4
ROPE_BASE = 10000.0
SSD_HEADDIM = 64
SSD_GROUPS = 4
SSD_STATE = 128
SSD_CONV = 4

FFN_TILE = 1024
FFN_SUB = 256
GLA_TILE = 1024
RET_TILE = 1024
SSD_TILE = 256
MIX_SUB = 256


def _dot(a, b):
    return jnp.dot(a, b, preferred_element_type=F32)


def _dot_nt(a, b):
    return lax.dot_general(a, b, (((1,), (1,)), ((), ())), preferred_element_type=F32)


def _dot_tn(a, b):
    return lax.dot_general(a, b, (((0,), (0,)), ((), ())), preferred_element_type=F32)


def _split_bf16(x, parts):
    out = []
    r = x
    for i in range(parts):
        p = r.astype(BF)
        out.append(p)
        if i + 1 < parts:
            r = r - p.astype(F32)
    return out


def _sel_dot_lhs(sel, x, parts):
    acc = None
    for p in _split_bf16(x, parts):
        t = _dot(sel, p)
        acc = t if acc is None else acc + t
    return acc


def _lane_group_pieces(x, group):
    grp = lax.broadcasted_iota(jnp.int32, x.shape, 1) // group
    p0 = x.astype(BF).astype(F32)
    r1 = x - p0
    p1 = r1.astype(BF).astype(F32)
    r2 = r1 - p1
    return jnp.where(grp == 0, p0, jnp.where(grp == 1, p1, jnp.where(grp == 2, r2, 0.0))).astype(BF)


def _sigmoid(x):
    return 1.0 / (1.0 + jnp.exp(-x))


def _silu(x):
    return x * _sigmoid(x)


def _softplus(x):
    return jnp.maximum(x, 0.0) + jnp.log(1.0 + jnp.exp(-jnp.abs(x)))


def _layer_norm(z, g, b):
    mu = jnp.mean(z, axis=-1, keepdims=True)
    zc = z - mu
    var = jnp.mean(zc * zc, axis=-1, keepdims=True)
    return zc * lax.rsqrt(var + LN_EPS) * g + b


def _chunk_causal(tile, chunk):
    ri = lax.broadcasted_iota(jnp.int32, (tile, tile), 0)
    ci = lax.broadcasted_iota(jnp.int32, (tile, tile), 1)
    return ((ri // chunk) == (ci // chunk)) & (ci <= ri)


def _chunk_tril(tile, chunk):
    return jnp.where(_chunk_causal(tile, chunk), 1.0, 0.0).astype(BF)


def _resident(shape):
    nd = len(shape)
    return pl.BlockSpec(shape, lambda *_: (0,) * nd, pipeline_mode=pl.Buffered(1))


def _ffn_body(x_ref, w_in_ref, w_out_ref, g_ref, b_ref, o_ref, *, d_ff, sub):
    for s in range(x_ref.shape[0] // sub):
        rows = slice(s * sub, (s + 1) * sub)
        x = x_ref[rows, :]
        gu = _dot(x.astype(BF), w_in_ref[...])
        act = (_silu(gu[:, :d_ff]) * gu[:, d_ff:]).astype(BF)
        y = _dot(act, w_out_ref[...])
        o_ref[rows, :] = _layer_norm(DN_ALPHA * x + 0.5 * y, g_ref[...], b_ref[...])


def _ffn_layer(h, w_in_all, w_out_all, layer, half, g, b):
    t, d = h.shape
    d_ff = w_out_all.shape[2]
    tm = min(FFN_TILE, t)
    pick = lambda *_: (layer, half, 0, 0)
    return pl.pallas_call(
        functools.partial(_ffn_body, d_ff=d_ff, sub=min(FFN_SUB, tm)),
        grid=(t // tm,),
        in_specs=[
            pl.BlockSpec((tm, d), lambda i: (i, 0)),
            pl.BlockSpec((None, None, d, 2 * d_ff), pick, pipeline_mode=pl.Buffered(1)),
            pl.BlockSpec((None, None, d_ff, d), pick, pipeline_mode=pl.Buffered(1)),
            _resident((1, d)),
            _resident((1, d)),
        ],
        out_specs=pl.BlockSpec((tm, d), lambda i: (i, 0)),
        out_shape=jax.ShapeDtypeStruct((t, d), F32),
        compiler_params=pltpu.CompilerParams(
            dimension_semantics=("parallel",), vmem_limit_bytes=VMEM_LIMIT),
        name="ffn_ln",
    )(h, w_in_all, w_out_all, g.reshape(1, d), b.reshape(1, d))


def _gla_body(x_ref, w_in_ref, w_gate_ref, b_gate_ref, ng_ref, w_out_ref, lg_ref, lb_ref,
              o_ref, st_ref, ov_ref, *, key, val, heads, sub):
    @pl.when(pl.program_id(1) == 0)
    def _():
        st_ref[...] = jnp.zeros_like(st_ref)

    def finish(rows, x, y):
        m = _dot(y, w_out_ref[...])
        o_ref[rows, :] = _layer_norm(DN_ALPHA * x + m, lg_ref[...], lb_ref[...])

    pending = None
    for s in range(x_ref.shape[0] // sub):
        rows = slice(s * sub, (s + 1) * sub)
        cur = (rows,) + _gla_rows(x_ref.at[rows, :], w_in_ref, w_gate_ref, b_gate_ref, ng_ref, st_ref,
                                  ov_ref.at[rows, :], key=key, val=val, heads=heads)
        if pending is not None:
            finish(*pending)
        pending = cur
    finish(*pending)


def _gla_rows(x_ref, w_in_ref, w_gate_ref, b_gate_ref, ng_ref, st_ref, ov_ref, *, key, val, heads):
    dk = key // heads
    dv = val // heads
    tile = x_ref.shape[0]

    x = x_ref[...]
    xb = x.astype(BF)
    g_low = _dot(xb, w_in_ref[:, 2 * key + 2 * val:2 * key + 2 * val + LANES]).astype(BF)
    q = _dot(xb, w_in_ref[:, 0:key]) * (dk ** -0.5)
    zg = _dot(g_low, w_gate_ref[...]) + b_gate_ref[...]
    k = _dot(xb, w_in_ref[:, key:2 * key])
    log_a = (jnp.minimum(zg, 0.0) - jnp.log(1.0 + jnp.exp(-jnp.abs(zg)))) * (1.0 / GLA_TAU)
    causal = _chunk_causal(tile, CHUNK)
    cum = _sel_dot_lhs(jnp.where(causal, 1.0, 0.0).astype(BF), log_a, 3)
    v = _dot(xb, w_in_ref[:, 2 * key:2 * key + val]).astype(BF)

    n_chunks = tile // CHUNK
    tots = [cum[(c + 1) * CHUNK - 1:(c + 1) * CHUNK, :] for c in range(n_chunks)]
    tot_rows = jnp.concatenate([jnp.broadcast_to(t, (CHUNK, key)) for t in tots], axis=0)
    q_f = q * jnp.exp(cum)
    q_dec = q_f.astype(BF)
    k_inv = (k * jnp.exp(-cum)).astype(BF)
    k_f = k * jnp.exp(tot_rows - cum)

    for h in range(heads):
        kl = slice(h * dk, (h + 1) * dk)
        vl = slice(h * dv, (h + 1) * dv)
        s = jnp.where(causal, _dot_nt(q_dec[:, kl], k_inv[:, kl]), 0.0).astype(BF)
        ov_ref[:, vl] = _dot(s, v[:, vl])

    head_of_lane = lax.broadcasted_iota(jnp.int32, (CHUNK, key), 1) // dk
    st = st_ref[...]
    for c in range(n_chunks):
        rows = slice(c * CHUNK, (c + 1) * CHUNK)
        qc = q_f[rows]
        kc = k_f[rows]
        qs = jnp.concatenate([jnp.where(head_of_lane == h, qc, 0.0) for h in range(heads)], axis=0).astype(BF)
        ks = jnp.concatenate([jnp.where(head_of_lane == h, kc, 0.0) for h in range(heads)], axis=0).astype(BF)
        vs = jnp.concatenate([v[rows, h * dv:(h + 1) * dv] for h in range(heads)], axis=0)
        o_int = _dot_nt(qs, st.astype(BF))
        for h in range(heads):
            ov_ref[rows, h * dv:(h + 1) * dv] += o_int[h * CHUNK:(h + 1) * CHUNK, :]
        st = st * jnp.exp(tots[c]) + _dot_tn(vs, ks)
    st_ref[...] = st

    r = _dot(xb, w_in_ref[:, 2 * key + val:2 * key + 2 * val])
    ng = ng_ref[...]
    normed = []
    for h in range(heads):
        oh = ov_ref[:, h * dv:(h + 1) * dv]
        ms = jnp.mean(oh * oh, axis=-1, keepdims=True)
        normed.append(oh * lax.rsqrt(ms + NORM_EPS) * ng)
    y = (jnp.concatenate(normed, axis=1) * _silu(r)).astype(BF)
    return x, y


def _gla_layer(h, w_in, w_gate, b_gate, norm_g, w_out, ln_g, ln_b):
    b, l, d = h.shape
    rank, key = w_gate.shape
    val = w_out.shape[0]
    heads = GLA_HEADS
    tile = min(GLA_TILE, l)
    w_in_p = jnp.pad(w_in, ((0, 0), (0, LANES - rank))).astype(BF)
    w_gate_p = jnp.pad(w_gate, ((0, LANES - rank), (0, 0))).astype(BF)
    n_in = w_in_p.shape[1]
    body = functools.partial(_gla_body, key=key, val=val, heads=heads, sub=min(MIX_SUB, tile))
    return pl.pallas_call(
        body,
        grid=(b, l // tile),
        in_specs=[
            pl.BlockSpec((None, tile, d), lambda i, j: (i, j, 0)),
            _resident((d, n_in)),
            _resident((LANES, key)),
            _resident((1, key)),
            _resident((1, val // heads)),
            _resident((val, d)),
            _resident((1, d)),
            _resident((1, d)),
        ],
        out_specs=pl.BlockSpec((None, tile, d), lambda i, j: (i, j, 0)),
        out_shape=jax.ShapeDtypeStruct((b, l, d), F32),
        scratch_shapes=[
            pltpu.VMEM((val // heads, key), F32),
            pltpu.VMEM((tile, val), F32),
        ],
        compiler_params=pltpu.CompilerParams(
            dimension_semantics=("parallel", "arbitrary"), vmem_limit_bytes=VMEM_LIMIT),
        name="gla_ln",
    )(h, w_in_p, w_gate_p, b_gate.reshape(1, key), norm_g.reshape(1, -1), w_out.astype(BF),
      ln_g.reshape(1, d), ln_b.reshape(1, d))


def _ret_body(x_ref, cos_ref, sin_ref, w_in_ref, ng_ref, w_out_ref, lg_ref, lb_ref,
              o_ref, st_ref, ov_ref, *, key, val, heads, sub):
    @pl.when(pl.program_id(1) == 0)
    def _():
        st_ref[...] = jnp.zeros_like(st_ref)

    for s in range(x_ref.shape[0] // sub):
        rows = slice(s * sub, (s + 1) * sub)
        _ret_rows(x_ref.at[rows, :], cos_ref.at[rows, :], sin_ref.at[rows, :], w_in_ref, ng_ref, w_out_ref,
                  lg_ref, lb_ref, o_ref.at[rows, :], st_ref, ov_ref.at[rows, :], key=key, val=val, heads=heads)


def _ret_rows(x_ref, cos_ref, sin_ref, w_in_ref, ng_ref, w_out_ref, lg_ref, lb_ref,
              o_ref, st_ref, ov_ref, *, key, val, heads):
    dk = key // heads
    dv = val // heads
    half = dk // 2
    tile = x_ref.shape[0]

    x = x_ref[...]
    xb = x.astype(BF)
    cos = cos_ref[...]
    sin = sin_ref[...]
    ri = lax.broadcasted_iota(jnp.int32, (tile, tile), 0)
    ci = lax.broadcasted_iota(jnp.int32, (tile, tile), 1)
    causal = ci <= ri
    dist = (ri - ci).astype(F32)
    pos = lax.broadcasted_iota(jnp.int32, (tile, 1), 0).astype(F32)
    ng = ng_ref[...]

    def rope(t):
        t1 = t[:, :half]
        t2 = t[:, half:]
        return jnp.concatenate([t1 * cos - t2 * sin, t1 * sin + t2 * cos], axis=1)

    for h in range(heads):
        log_g = math.log(1.0 - 2.0 ** (-5.0 - h))
        qh = rope(_dot(xb, w_in_ref[:, h * dk:(h + 1) * dk]))
        kh = rope(_dot(xb, w_in_ref[:, key + h * dk:key + (h + 1) * dk])) * (dk ** -0.5)
        vh = _dot(xb, w_in_ref[:, 2 * key + h * dv:2 * key + (h + 1) * dv]).astype(BF)
        qb = qh.astype(BF)
        decay = jnp.where(causal, jnp.exp(dist * log_g), 0.0)
        s = (_dot_nt(qb, kh.astype(BF)) * decay).astype(BF)
        st = st_ref[h]
        o = _dot(s, vh) + _dot(qb, st.astype(BF)) * jnp.exp((pos + 1.0) * log_g)
        k_dec = (kh * jnp.exp((tile - 1.0 - pos) * log_g)).astype(BF)
        st_ref[h] = math.exp(tile * log_g) * st + _dot_tn(k_dec, vh)
        mu = jnp.mean(o, axis=-1, keepdims=True)
        oc = o - mu
        var = jnp.mean(oc * oc, axis=-1, keepdims=True)
        ov_ref[:, h * dv:(h + 1) * dv] = oc * lax.rsqrt(var + NORM_EPS) * ng

    g = _dot(xb, w_in_ref[:, 2 * key + val:2 * key + 2 * val])
    y = (ov_ref[...] * _silu(g)).astype(BF)
    m = _dot(y, w_out_ref[...])
    o_ref[...] = _layer_norm(DN_ALPHA * x + m, lg_ref[...], lb_ref[...])


def _ret_layer(h, w_in, norm_g, w_out, ln_g, ln_b):
    b, l, d = h.shape
    val = w_out.shape[0]
    key = (w_in.shape[1] - 2 * val) // 2
    heads = RET_HEADS
    dk = key // heads
    tile = min(RET_TILE, l)
    inv = ROPE_BASE ** (-jnp.arange(0, dk, 2, dtype=F32) / dk)
    ang = jnp.arange(l, dtype=F32)[:, None] * inv[None, :]
    cos = jnp.cos(ang)
    sin = jnp.sin(ang)
    body = functools.partial(_ret_body, key=key, val=val, heads=heads, sub=min(MIX_SUB, tile))
    return pl.pallas_call(
        body,
        grid=(b, l // tile),
        in_specs=[
            pl.BlockSpec((None, tile, d), lambda i, j: (i, j, 0)),
            pl.BlockSpec((tile, dk // 2), lambda i, j: (j, 0)),
            pl.BlockSpec((tile, dk // 2), lambda i, j: (j, 0)),
            _resident((d, w_in.shape[1])),
            _resident((1, val // heads)),
            _resident((val, d)),
            _resident((1, d)),
            _resident((1, d)),
        ],
        out_specs=pl.BlockSpec((None, tile, d), lambda i, j: (i, j, 0)),
        out_shape=jax.ShapeDtypeStruct((b, l, d), F32),
        scratch_shapes=[
            pltpu.VMEM((heads, dk, val // heads), F32),
            pltpu.VMEM((tile, val), F32),
        ],
        compiler_params=pltpu.CompilerParams(
            dimension_semantics=("parallel", "arbitrary"), vmem_limit_bytes=VMEM_LIMIT),
        name="ret_ln",
    )(h, cos, sin, w_in.astype(BF), norm_g.reshape(1, -1), w_out.astype(BF),
      ln_g.reshape(1, d), ln_b.reshape(1, d))


def _ssd_body(x_ref, w_in_ref, cw_ref, cb_ref, dtb_ref, alog_ref, dsk_ref, ng_ref, ex_ref,
              w_out_ref, lg_ref, lb_ref, o_ref, st_ref, cbuf_ref, y_ref, *, inner, groups, state, sub):
    tile = x_ref.shape[0]
    conv_dim = inner + 2 * groups * state

    @pl.when(pl.program_id(1) == 0)
    def _():
        st_ref[...] = jnp.zeros_like(st_ref)
        cbuf_ref[0:SUBLANES, :] = jnp.zeros((SUBLANES, conv_dim), F32)

    for s in range(tile // sub):
        rows = slice(s * sub, (s + 1) * sub)
        x, y = _ssd_rows(x_ref.at[rows, :], w_in_ref, cw_ref, cb_ref, dtb_ref, alog_ref, dsk_ref,
                         ng_ref, ex_ref, st_ref, cbuf_ref.at[s * sub:(s + 1) * sub + SUBLANES, :],
                         y_ref.at[rows, :], inner=inner, groups=groups, state=state)
        m = _dot(y, w_out_ref[...])
        o_ref[rows, :] = _layer_norm(DN_ALPHA * x + m, lg_ref[...], lb_ref[...])
    cbuf_ref[0:SUBLANES, :] = cbuf_ref[tile:tile + SUBLANES, :]


def _ssd_rows(x_ref, w_in_ref, cw_ref, cb_ref, dtb_ref, alog_ref, dsk_ref, ng_ref, ex_ref,
              st_ref, cbuf_ref, y_ref, *, inner, groups, state):
    tile = x_ref.shape[0]
    gw = inner // groups
    pair = 2 * SSD_HEADDIM
    conv_dim = inner + 2 * groups * state

    x = x_ref[...]
    xb = x.astype(BF)

    dt_raw = _dot(xb, w_in_ref[:, inner + conv_dim:inner + conv_dim + LANES])
    cbuf_ref[SUBLANES:SUBLANES + tile, :] = _dot(xb, w_in_ref[:, inner:inner + conv_dim])
    dt_h = _softplus(dt_raw + dtb_ref[...])
    da_h = dt_h * (-jnp.exp(alog_ref[...]))
    cum_h = _sel_dot_lhs(_chunk_tril(tile, CHUNK), da_h, 3)
    z = _dot(xb, w_in_ref[:, 0:inner])
    n_heads = inner // SSD_HEADDIM
    dt = _dot(_lane_group_pieces(dt_h, n_heads), ex_ref[...])
    cum = _dot(_lane_group_pieces(cum_h, n_heads), ex_ref[...])

    win = cbuf_ref[...]
    win1 = pltpu.roll(win, 1, 0)
    pair_b = cw_ref[1:2, :] * win + cw_ref[0:1, :] * win1
    pair_a = cw_ref[3:4, :] * win[SUBLANES:] + cw_ref[2:3, :] * win1[SUBLANES:]
    acc = pair_a + pltpu.roll(pair_b, 2, 0)[SUBLANES:] + cb_ref[...]
    xbc = _silu(acc)
    xs = xbc[:, :inner]
    bm = xbc[:, inner:inner + groups * state].astype(BF)
    cm = xbc[:, inner + groups * state:].astype(BF)
    xdt = xs * dt

    ri = lax.broadcasted_iota(jnp.int32, (CHUNK, pair), 0)
    ci = lax.broadcasted_iota(jnp.int32, (CHUNK, pair), 1)
    cj = ci % CHUNK
    causal = cj <= ri
    diag = cj == ri
    lane = lax.broadcasted_iota(jnp.int32, (CHUNK, pair), 1)
    first = lane < SSD_HEADDIM
    dsk = dsk_ref[...]

    n_chunks = tile // CHUNK
    cb_all = []
    for g in range(groups):
        sl = slice(g * state, (g + 1) * state)
        b_twice = jnp.concatenate(
            [bm[c * CHUNK:(c + 1) * CHUNK, sl] for c in range(n_chunks) for _ in range(2)], axis=0)
        cb_all.append(_dot_nt(cm[:, sl], b_twice))

    for c in range(n_chunks):
        rows = slice(c * CHUNK, (c + 1) * CHUNK)
        cum_c = cum[rows]
        last = cum_c[CHUNK - 1:CHUNK, :]
        e_cum = jnp.exp(cum_c)
        x_end = (xdt[rows] * jnp.exp(last - cum_c)).astype(BF)
        dec = jnp.exp(last)
        for g in range(groups):
            sl = slice(g * state, (g + 1) * state)
            gl = slice(g * gw, (g + 1) * gw)
            cg = cm[rows, sl]
            bg = bm[rows, sl]
            cb2 = cb_all[g][rows, c * pair:(c + 1) * pair]
            st = st_ref[g]
            y_state = _dot(cg, st.astype(BF)) * e_cum[:, gl]
            st_ref[g] = st * dec[:, gl] + _dot_tn(bg, x_end[:, gl])
            ys = []
            for p in range(gw // pair):
                pl_ = slice(g * gw + p * pair, g * gw + (p + 1) * pair)
                col = cum_c[:, pl_]
                row = jnp.sum(jnp.where(diag, col, 0.0), axis=0, keepdims=True)
                seg = jnp.exp(jnp.where(causal, col - row, -1e30))
                w = (seg * cb2).astype(BF)
                xp = xdt[rows, pl_].astype(BF)
                zero = jnp.zeros_like(xp)
                rhs = jnp.concatenate([jnp.where(first, xp, zero), jnp.where(first, zero, xp)], axis=0)
                ys.append(_dot(w, rhs))
            y_ref[rows, gl] = jnp.concatenate(ys, axis=1) + y_state + xs[rows, gl] * dsk[:, gl]

    ng = ng_ref[...]
    normed = []
    for g in range(groups):
        gl = slice(g * gw, (g + 1) * gw)
        yg = y_ref[:, gl] * _silu(z[:, gl])
        ms = jnp.mean(yg * yg, axis=-1, keepdims=True)
        normed.append((yg * lax.rsqrt(ms + NORM_EPS) * ng[:, gl]).astype(BF))
    return x, jnp.concatenate(normed, axis=1)


def _ssd_layer(h, w_in, conv_w, conv_b, dt_bias, a_log, d_skip, norm_g, w_out, ln_g, ln_b):
    b, l, d = h.shape
    inner = w_out.shape[0]
    heads = dt_bias.shape[0]
    groups = SSD_GROUPS
    state = SSD_STATE
    conv_dim = conv_w.shape[1]
    assert inner // heads == SSD_HEADDIM and conv_dim == inner + 2 * groups * state
    tile = min(SSD_TILE, l)
    reps = LANES // heads
    assert reps * heads == LANES and reps >= 3
    w_in_p = jnp.concatenate([w_in[:, :inner + conv_dim]] + [w_in[:, inner + conv_dim:]] * reps, axis=1).astype(BF)
    lane_pad = lambda t: jnp.tile(t, reps).reshape(1, LANES)
    lane_head = jnp.where(jnp.arange(LANES) < 3 * heads, jnp.arange(LANES) % heads, -1)
    expand = (lane_head[:, None] == (jnp.arange(inner) // SSD_HEADDIM)[None, :]).astype(BF)
    rep = lambda t: jnp.repeat(t, SSD_HEADDIM).reshape(1, inner)
    body = functools.partial(_ssd_body, inner=inner, groups=groups, state=state, sub=min(MIX_SUB, tile))
    return pl.pallas_call(
        body,
        grid=(b, l // tile),
        in_specs=[
            pl.BlockSpec((None, tile, d), lambda i, j: (i, j, 0)),
            _resident((d, w_in_p.shape[1])),
            _resident((SSD_CONV, conv_dim)),
            _resident((1, conv_dim)),
            _resident((1, LANES)),
            _resident((1, LANES)),
            _resident((1, inner)),
            _resident((1, inner)),
            _resident((LANES, inner)),
            _resident((inner, d)),
            _resident((1, d)),
            _resident((1, d)),
        ],
        out_specs=pl.BlockSpec((None, tile, d), lambda i, j: (i, j, 0)),
        out_shape=jax.ShapeDtypeStruct((b, l, d), F32),
        scratch_shapes=[
            pltpu.VMEM((groups, state, inner // groups), F32),
            pltpu.VMEM((SUBLANES + tile, conv_dim), F32),
            pltpu.VMEM((tile, inner), F32),
        ],
        compiler_params=pltpu.CompilerParams(
            dimension_semantics=("parallel", "arbitrary"), vmem_limit_bytes=VMEM_LIMIT),
        name="ssd_ln",
    )(h, w_in_p, conv_w, conv_b.reshape(1, conv_dim), lane_pad(dt_bias), lane_pad(a_log), rep(d_skip),
      norm_g.reshape(1, inner), expand, w_out.astype(BF), ln_g.reshape(1, d), ln_b.reshape(1, d))


def kernel(x, ffn_w_in, ffn_w_out, ln_g, ln_b, gla_w_in, gla_w_gate, gla_b_gate, gla_norm_g, gla_w_out,
           ret_w_in, ret_norm_g, ret_w_out, ssd_w_in, ssd_conv_w, ssd_conv_b, ssd_dt_bias, ssd_a_log,
           ssd_d, ssd_norm_g, ssd_w_out):
    b, l, d = x.shape
    h = x
    ffn_w_in = ffn_w_in.astype(BF)
    ffn_w_out = ffn_w_out.astype(BF)
    for i in range(DEPTH):
        h = _ffn_layer(h.reshape(b * l, d), ffn_w_in, ffn_w_out, i, 0, ln_g[i, 0], ln_b[i, 0])
        h = h.reshape(b, l, d)
        kind, j = i % N_MIXERS, i // N_MIXERS
        if kind == 0:
            h = _gla_layer(h, gla_w_in[j], gla_w_gate[j], gla_b_gate[j], gla_norm_g[j], gla_w_out[j],
                           ln_g[i, 1], ln_b[i, 1])
        elif kind == 1:
            h = _ret_layer(h, ret_w_in[j], ret_norm_g[j], ret_w_out[j], ln_g[i, 1], ln_b[i, 1])
        else:
            h = _ssd_layer(h, ssd_w_in[j], ssd_conv_w[j], ssd_conv_b[j], ssd_dt_bias[j], ssd_a_log[j],
                           ssd_d[j], ssd_norm_g[j], ssd_w_out[j], ln_g[i, 1], ln_b[i, 1])
        h = _ffn_layer(h.reshape(b * l, d), ffn_w_in, ffn_w_out, i, 1, ln_g[i, 2], ln_b[i, 2])
        h = h.reshape(b, l, d)
    return h
```
